```python
import math
import jax, jax.numpy as jnp
from jax import lax
import numpy as np


D_MODEL = 2048
BATCH = 2
SEQ = 16384
DEPTH = 1
DEC_BATCH = 1
DEC_SEQ = 8192
PAST_LEN = 128

N_HEADS = 8
HEAD_DIM = 128
V_HEAD_DIM = 2 * HEAD_DIM
QK_WIDTH = 2 * N_HEADS * HEAD_DIM
V_WIDTH = N_HEADS * V_HEAD_DIM
CONV_CH = 1024
CONV_WIDTH = 31
N_BRANCH = 2
FFN_DIM = 5632
FFN_CONV_WIDTH = 3
ROPE_THETA = 10000.0
Q_BLOCK = 128
EPS = 1e-6
IN_WIDTH = 2 * QK_WIDTH + V_WIDTH + 2 * CONV_CH + N_BRANCH * D_MODEL
SPLITS = (QK_WIDTH, 2 * QK_WIDTH, 2 * QK_WIDTH + V_WIDTH, 2 * QK_WIDTH + V_WIDTH + 2 * CONV_CH)

kernel_name = 'hybrid_diffattn_conformer_encoder'


def rms_norm(x, g):
    xf = x.astype(jnp.float32)
    y = xf * lax.rsqrt(jnp.mean(xf * xf, axis=-1, keepdims=True) + EPS)
    return (y * g.astype(jnp.float32)).astype(x.dtype)


def layer_norm(x, g, b):
    xf = x.astype(jnp.float32)
    xc = xf - jnp.mean(xf, axis=-1, keepdims=True)
    y = xc * lax.rsqrt(jnp.mean(xc * xc, axis=-1, keepdims=True) + EPS)
    return (y * g.astype(jnp.float32) + b.astype(jnp.float32)).astype(x.dtype)


def rope_tables(seq, dtype):
    inv_freq = 1.0 / (ROPE_THETA ** (jnp.arange(0, HEAD_DIM, 2, dtype=jnp.float32) / HEAD_DIM))
    ang = jnp.arange(seq, dtype=jnp.float32)[:, None] * inv_freq[None, :]
    ang = jnp.concatenate([ang, ang], axis=-1)[:, None, :]
    return jnp.cos(ang).astype(dtype), jnp.sin(ang).astype(dtype)


def apply_rope(x, cos, sin):
    x1, x2 = jnp.split(x, 2, axis=-1)
    return x * cos + jnp.concatenate([-x2, x1], axis=-1) * sin


def depthwise_conv(x, w, b):
    pad = (w.shape[0] - 1) // 2
    y = lax.conv_general_dilated(x, w[:, None, :].astype(x.dtype), window_strides=(1,),
                                 padding=[(pad, pad)], dimension_numbers=('NWC', 'WIO', 'NWC'),
                                 feature_group_count=x.shape[-1])
    return y + b.astype(x.dtype)


def diff_attention(q, k, v, lam):
    b, s = q.shape[0], q.shape[1]
    nb = s // Q_BLOCK
    scale = HEAD_DIM ** -0.5
    qb = q.reshape(b, nb, Q_BLOCK, 2 * N_HEADS, HEAD_DIM).transpose(1, 0, 3, 2, 4)
    kt = k.transpose(0, 2, 1, 3)
    vt = v.transpose(0, 2, 1, 3).astype(jnp.float32)

    def block(q_blk):
        sc = jnp.einsum('bhqd,bhkd->bhqk', q_blk, kt, preferred_element_type=jnp.float32) * scale
        p = jax.nn.softmax(sc, axis=-1).reshape(b, N_HEADS, 2, Q_BLOCK, s)
        a = p[:, :, 0] - lam * p[:, :, 1]
        return jnp.einsum('bhqk,bhkv->bhqv', a, vt)

    o = lax.map(block, qb)
    return o.transpose(1, 0, 3, 2, 4).reshape(b, s, N_HEADS, V_HEAD_DIM).astype(q.dtype)


def encoder_layer(x, layer_idx, norm1_g, w_in, b_gate, q_norm_g, k_norm_g, lambda_q1, lambda_k1,
                  lambda_q2, lambda_k2, subln_g, w_attn_proj, conv_dw_w, conv_dw_b, conv_ln_g,
                  conv_ln_b, w_conv_proj, w_out, norm2_g, w_up, ffn_dw_w, ffn_dw_b, w_down):
    b, s, _ = x.shape
    h = rms_norm(x, norm1_g)
    z = h @ w_in
    q, k, v, u, g = jnp.split(z, SPLITS, axis=-1)

    cos, sin = rope_tables(s, x.dtype)
    q = apply_rope(rms_norm(q.reshape(b, s, 2 * N_HEADS, HEAD_DIM), q_norm_g), cos, sin)
    k = apply_rope(rms_norm(k.reshape(b, s, 2 * N_HEADS, HEAD_DIM), k_norm_g), cos, sin)
    v = v.reshape(b, s, N_HEADS, V_HEAD_DIM)
    lam_init = 0.8 - 0.6 * math.exp(-0.3 * layer_idx)
    f32 = jnp.float32
    lam = (jnp.exp(jnp.sum(lambda_q1.astype(f32) * lambda_k1.astype(f32)))
           - jnp.exp(jnp.sum(lambda_q2.astype(f32) * lambda_k2.astype(f32))) + lam_init)
    o = diff_attention(q, k, v, lam)
    o = rms_norm(o, subln_g) * (1.0 - lam_init)
    a_branch = o.reshape(b, s, V_WIDTH) @ w_attn_proj

    u_a, u_b = jnp.split(u, 2, axis=-1)
    c = depthwise_conv(u_a * jax.nn.sigmoid(u_b), conv_dw_w, conv_dw_b)
    c = jax.nn.silu(layer_norm(c, conv_ln_g, conv_ln_b))
    c_branch = c @ w_conv_proj

    gates = jax.nn.sigmoid(g + b_gate).reshape(b, s, N_BRANCH, D_MODEL)
    m = gates[:, :, 0] * a_branch + gates[:, :, 1] * c_branch
    x = x + m @ w_out

    up = depthwise_conv(rms_norm(x, norm2_g) @ w_up, ffn_dw_w, ffn_dw_b)
    f_a, f_b = jnp.split(up, 2, axis=-1)
    return x + (jax.nn.silu(f_a) * f_b) @ w_down


def setup_inputs(seed: int = 0) -> dict:
    key = jax.random.key(seed)
    ks = jax.random.split(key, 32)

    def nrm(k, shape, scale):
        return jax.random.normal(k, shape, dtype=jnp.float32) * scale

    def gain(k, n):
        return 1.0 + nrm(k, (DEPTH, n), 0.02)

    return {
        'x_prompt': nrm(ks[0], (BATCH, SEQ, D_MODEL), 1.0),
        'x_sample': nrm(ks[1], (DEC_BATCH, DEC_SEQ, D_MODEL), 1.0),
        'norm1_g': gain(ks[2], D_MODEL),
        'w_in': nrm(ks[3], (DEPTH, D_MODEL, IN_WIDTH), D_MODEL ** -0.5),
        'b_gate': nrm(ks[4], (DEPTH, N_BRANCH * D_MODEL), 0.02),
        'q_norm_g': gain(ks[5], HEAD_DIM),
        'k_norm_g': gain(ks[6], HEAD_DIM),
        'lambda_q1': nrm(ks[7], (DEPTH, HEAD_DIM), 0.1),
        'lambda_k1': nrm(ks[8], (DEPTH, HEAD_DIM), 0.1),
        'lambda_q2': nrm(ks[9], (DEPTH, HEAD_DIM), 0.1),
        'lambda_k2': nrm(ks[10], (DEPTH, HEAD_DIM), 0.1),
        'subln_g': gain(ks[11], V_HEAD_DIM),
        'w_attn_proj': nrm(ks[12], (DEPTH, V_WIDTH, D_MODEL), V_WIDTH ** -0.5),
        'conv_dw_w': nrm(ks[13], (DEPTH, CONV_WIDTH, CONV_CH), CONV_WIDTH ** -0.5),
        'conv_dw_b': nrm(ks[14], (DEPTH, CONV_CH), 0.02),
        'conv_ln_g': gain(ks[15], CONV_CH),
        'conv_ln_b': nrm(ks[16], (DEPTH, CONV_CH), 0.02),
        'w_conv_proj': nrm(ks[17], (DEPTH, CONV_CH, D_MODEL), CONV_CH ** -0.5),
        'w_out': nrm(ks[18], (DEPTH, D_MODEL, D_MODEL), D_MODEL ** -0.5),
        'norm2_g': gain(ks[19], D_MODEL),
        'w_up': nrm(ks[20], (DEPTH, D_MODEL, 2 * FFN_DIM), D_MODEL ** -0.5),
        'ffn_dw_w': nrm(ks[21], (DEPTH, FFN_CONV_WIDTH, 2 * FFN_DIM), FFN_CONV_WIDTH ** -0.5),
        'ffn_dw_b': nrm(ks[22], (DEPTH, 2 * FFN_DIM), 0.02),
        'w_down': nrm(ks[23], (DEPTH, FFN_DIM, D_MODEL), FFN_DIM ** -0.5),
    }


def reference(x_prompt, x_sample, norm1_g, w_in, b_gate, q_norm_g, k_norm_g, lambda_q1, lambda_k1,
              lambda_q2, lambda_k2, subln_g, w_attn_proj, conv_dw_w, conv_dw_b, conv_ln_g, conv_ln_b,
              w_conv_proj, w_out, norm2_g, w_up, ffn_dw_w, ffn_dw_b, w_down):
    def run(x):
        for l in range(DEPTH):
            x = encoder_layer(x, l, norm1_g[l], w_in[l], b_gate[l], q_norm_g[l], k_norm_g[l],
                              lambda_q1[l], lambda_k1[l], lambda_q2[l], lambda_k2[l], subln_g[l],
                              w_attn_proj[l], conv_dw_w[l], conv_dw_b[l], conv_ln_g[l], conv_ln_b[l],
                              w_conv_proj[l], w_out[l], norm2_g[l], w_up[l], ffn_dw_w[l],
                              ffn_dw_b[l], w_down[l])
        return x

    y_prompt = run(x_prompt)
    y_sample = run(x_sample)
    return (y_prompt, y_sample)
```

```python
import functools
import math

import jax
import jax.numpy as jnp
from jax import lax
from jax.experimental import pallas as pl
from jax.experimental.pallas import tpu as pltpu

F32 = jnp.float32
BF16 = jnp.bfloat16

D_MODEL = 2048
N_HEADS = 8
HEAD_DIM = 128
V_HEAD_DIM = 2 * HEAD_DIM
QK_WIDTH = 2 * N_HEADS * HEAD_DIM
V_WIDTH = N_HEADS * V_HEAD_DIM
CONV_CH = 1024
CONV_WIDTH = 31
N_BRANCH = 2
FFN_DIM = 5632
FFN_CONV_WIDTH = 3
ROPE_THETA = 10000.0
EPS = 1e-6
IN_WIDTH = 2 * QK_WIDTH + V_WIDTH + 2 * CONV_CH + N_BRANCH * D_MODEL

VMEM_LIMIT_BYTES = 56 * 1024 * 1024
HALO_ROWS = 16
IN_COL_TILE = 1024
LOG2E = math.log2(math.e)


def _params(*semantics):
    return pltpu.CompilerParams(dimension_semantics=semantics, vmem_limit_bytes=VMEM_LIMIT_BYTES)


def _resident(shape):
    return pl.BlockSpec(shape, lambda *_: (0,) * len(shape), pipeline_mode=pl.Buffered(1))


def _row_tile(seq):
    return min(512, seq)


def _qk_epilogue(z, gain, cos, sin_signed, out_scale):
    outs = []
    for h in range(z.shape[1] // HEAD_DIM):
        zh = z[:, h * HEAD_DIM:(h + 1) * HEAD_DIM]
        ms = jnp.mean(zh * zh, axis=-1, keepdims=True)
        y = zh * lax.rsqrt(ms + EPS) * gain
        y = y * cos + pltpu.roll(y, HEAD_DIM // 2, 1) * sin_signed
        outs.append((y * out_scale).astype(BF16))
    return jnp.concatenate(outs, axis=1)


def _in_proj_kernel(x_ref, g1_ref, w_ref, bg_ref, qg_ref, kg_ref, cos_ref, sin_ref,
                    q_ref, k_ref, v_ref, glu_ref, gate_ref, h_ref, ua_ref, *, q_scale):
    j = pl.program_id(1)

    @pl.when(j == 0)
    def _():
        x = x_ref[...]
        ms = jnp.mean(x * x, axis=-1, keepdims=True)
        h_ref[...] = (x * lax.rsqrt(ms + EPS) * g1_ref[...]).astype(BF16)

    z = jnp.dot(h_ref[...], w_ref[...], preferred_element_type=F32)

    @pl.when(j < 2)
    def _():
        q_ref[...] = _qk_epilogue(z, qg_ref[...], cos_ref[...], sin_ref[...], q_scale)

    @pl.when((j >= 2) & (j < 4))
    def _():
        k_ref[...] = _qk_epilogue(z, kg_ref[...], cos_ref[...], sin_ref[...], 1.0)

    @pl.when((j >= 4) & (j < 6))
    def _():
        v_ref[...] = z.astype(BF16)

    @pl.when(j == 6)
    def _():
        ua_ref[...] = z

    @pl.when(j == 7)
    def _():
        glu_ref[...] = (ua_ref[...] * jax.nn.sigmoid(z)).astype(BF16)

    @pl.when(j >= 8)
    def _():
        gate_ref[...] = jax.nn.sigmoid(z + bg_ref[...]).astype(BF16)


def _in_proj(x2, seq, norm1_g, w_in, b_gate, q_norm_g, k_norm_g, cos, sin_signed):
    rows = x2.shape[0]
    tm = _row_tile(seq)
    tn = IN_COL_TILE
    seq_tiles = seq // tm

    def clamp(j, lo, n):
        return jnp.clip(j - lo, 0, n - 1)

    kernel = functools.partial(_in_proj_kernel, q_scale=HEAD_DIM ** -0.5 * LOG2E)
    out_shapes = (
        jax.ShapeDtypeStruct((rows, QK_WIDTH), BF16),
        jax.ShapeDtypeStruct((rows, QK_WIDTH), BF16),
        jax.ShapeDtypeStruct((rows, V_WIDTH), BF16),
        jax.ShapeDtypeStruct((rows, CONV_CH), BF16),
        jax.ShapeDtypeStruct((rows, N_BRANCH * D_MODEL), BF16),
    )
    return pl.pallas_call(
        kernel,
        grid=(rows // tm, IN_WIDTH // tn),
        in_specs=[
            pl.BlockSpec((tm, D_MODEL), lambda i, j: (i, 0)),
            pl.BlockSpec((1, D_MODEL), lambda i, j: (0, 0)),
            pl.BlockSpec((D_MODEL, tn), lambda i, j: (0, j)),
            pl.BlockSpec((1, tn), lambda i, j: (0, clamp(j, 8, 4))),
            pl.BlockSpec((1, HEAD_DIM), lambda i, j: (0, 0)),
            pl.BlockSpec((1, HEAD_DIM), lambda i, j: (0, 0)),
            pl.BlockSpec((tm, HEAD_DIM), lambda i, j: (i % seq_tiles, 0)),
            pl.BlockSpec((tm, HEAD_DIM), lambda i, j: (i % seq_tiles, 0)),
        ],
        out_specs=(
            pl.BlockSpec((tm, tn), lambda i, j: (i, clamp(j, 0, 2))),
            pl.BlockSpec((tm, tn), lambda i, j: (i, clamp(j, 2, 2))),
            pl.BlockSpec((tm, tn), lambda i, j: (i, clamp(j, 4, 2))),
            pl.BlockSpec((tm, CONV_CH), lambda i, j: (i, 0)),
            pl.BlockSpec((tm, tn), lambda i, j: (i, clamp(j, 8, 4))),
        ),
        out_shape=out_shapes,
        scratch_shapes=[pltpu.VMEM((tm, D_MODEL), BF16), pltpu.VMEM((tm, CONV_CH), F32)],
        compiler_params=_params("parallel", "arbitrary"),
        name="in_proj",
    )(x2, norm1_g, w_in, b_gate, q_norm_g, k_norm_g, cos, sin_signed)


def _attn_kernel(lq1_ref, lk1_ref, lq2_ref, lk2_ref, sg_ref, q_ref, k_ref, v_ref, o_ref,
                 m_ref, l_ref, acc_ref, *, k_chunk, lam_init):
    seq = k_ref.shape[0]
    m_ref[...] = jnp.full(m_ref.shape, -jnp.inf, F32)
    l_ref[...] = jnp.zeros(l_ref.shape, F32)
    acc_ref[...] = jnp.zeros(acc_ref.shape, F32)

    def body(c, carry):
        c0 = pl.multiple_of(c * k_chunk, k_chunk)
        v_blk = v_ref[pl.ds(c0, k_chunk), :]
        for sub in range(2):
            lanes = slice(sub * HEAD_DIM, (sub + 1) * HEAD_DIM)
            s = lax.dot_general(q_ref[:, lanes], k_ref[pl.ds(c0, k_chunk), lanes],
                                (((1,), (1,)), ((), ())), preferred_element_type=F32)
            m_old = m_ref[sub]
            m_new = jnp.maximum(m_old, jnp.max(s, axis=-1, keepdims=True))
            p = jnp.exp2(s - m_new)
            alpha = jnp.exp2(m_old - m_new)
            l_ref[sub] = alpha * l_ref[sub] + jnp.sum(p, axis=-1, keepdims=True)
            acc_ref[sub] = alpha * acc_ref[sub] + jnp.dot(p.astype(BF16), v_blk,
                                                          preferred_element_type=F32)
            m_ref[sub] = m_new
        return carry

    lax.fori_loop(0, seq // k_chunk, body, 0)

    lam = (jnp.exp(jnp.sum(lq1_ref[...] * lk1_ref[...], axis=-1, keepdims=True))
           - jnp.exp(jnp.sum(lq2_ref[...] * lk2_ref[...], axis=-1, keepdims=True)) + lam_init)
    o = acc_ref[0] / l_ref[0] - lam * (acc_ref[1] / l_ref[1])
    ms = jnp.mean(o * o, axis=-1, keepdims=True)
    o = o * lax.rsqrt(ms + EPS) * sg_ref[...] * (1.0 - lam_init)
    o_ref[...] = o.astype(BF16)


def _attention(q, k, v, lq1, lk1, lq2, lk2, subln_g, lam_init):
    batch, seq, _ = q.shape
    tq = min(512, seq)
    k_chunk = min(512, seq)
    vec = pl.BlockSpec((1, HEAD_DIM), lambda b, h, i: (0, 0))
    kernel = functools.partial(_attn_kernel, k_chunk=k_chunk, lam_init=lam_init)
    return pl.pallas_call(
        kernel,
        grid=(batch, N_HEADS, seq // tq),
        in_specs=[
            vec, vec, vec, vec,
            pl.BlockSpec((1, V_HEAD_DIM), lambda b, h, i: (0, 0)),
            pl.BlockSpec((None, tq, V_HEAD_DIM), lambda b, h, i: (b, i, h)),
            pl.BlockSpec((None, seq, V_HEAD_DIM), lambda b, h, i: (b, 0, h)),
            pl.BlockSpec((None, seq, V_HEAD_DIM), lambda b, h, i: (b, 0, h)),
        ],
        out_specs=pl.BlockSpec((None, tq, V_HEAD_DIM), lambda b, h, i: (b, i, h)),
        out_shape=jax.ShapeDtypeStruct((batch, seq, V_WIDTH), BF16),
        scratch_shapes=[
            pltpu.VMEM((2, tq, 1), F32),
            pltpu.VMEM((2, tq, 1), F32),
            pltpu.VMEM((2, tq, V_HEAD_DIM), F32),
        ],
        compiler_params=_params("parallel", "parallel", "arbitrary"),
        name="diff_attention",
    )(lq1, lk1, lq2, lk2, subln_g, q, k, v)


def _conv_kernel(prev_ref, cur_ref, next_ref, w_ref, b_ref, lg_ref, lb_ref, o_ref, ext_ref,
                 *, row_chunk):
    i = pl.program_id(1)
    tm = cur_ref.shape[0]
    pad = (CONV_WIDTH - 1) // 2
    ext_ref[0:HALO_ROWS] = jnp.where(i > 0, prev_ref[...].astype(F32), 0.0)
    ext_ref[HALO_ROWS:HALO_ROWS + tm] = cur_ref[...].astype(F32)
    ext_ref[HALO_ROWS + tm:] = jnp.where(i < pl.num_programs(1) - 1, next_ref[...].astype(F32), 0.0)

    def body(r, carry):
        r0 = pl.multiple_of(r * row_chunk, row_chunk)
        win = ext_ref[pl.ds(r0, row_chunk + 2 * HALO_ROWS), :]
        acc = jnp.broadcast_to(b_ref[...], (row_chunk, CONV_CH))
        for t in range(CONV_WIDTH):
            off = HALO_ROWS - pad + t
            acc = acc + w_ref[t:t + 1, :] * win[off:off + row_chunk, :]
        mu = jnp.mean(acc, axis=-1, keepdims=True)
        xc = acc - mu
        var = jnp.mean(xc * xc, axis=-1, keepdims=True)
        y = xc * lax.rsqrt(var + EPS) * lg_ref[...] + lb_ref[...]
        o_ref[pl.ds(r0, row_chunk), :] = (y * jax.nn.sigmoid(y)).astype(BF16)
        return carry

    lax.fori_loop(0, tm // row_chunk, body, 0)


def _halo_specs(tm, seq, width, col_map):
    per_tile = tm // HALO_ROWS
    last = seq // HALO_ROWS - 1
    prev = pl.BlockSpec((None, HALO_ROWS, width),
                        lambda b, i, *f: (b, jnp.maximum(i * per_tile - 1, 0), col_map(*f)))
    nxt = pl.BlockSpec((None, HALO_ROWS, width),
                       lambda b, i, *f: (b, jnp.minimum((i + 1) * per_tile, last), col_map(*f)))
    return prev, nxt


def _conv_module(glu, conv_dw_w, conv_dw_b, conv_ln_g, conv_ln_b):
    batch, seq, _ = glu.shape
    tm = _row_tile(seq)
    prev, nxt = _halo_specs(tm, seq, CONV_CH, lambda: 0)
    vec = pl.BlockSpec((1, CONV_CH), lambda b, i: (0, 0))
    return pl.pallas_call(
        functools.partial(_conv_kernel, row_chunk=16),
        grid=(batch, seq // tm),
        in_specs=[
            prev,
            pl.BlockSpec((None, tm, CONV_CH), lambda b, i: (b, i, 0)),
            nxt,
            pl.BlockSpec((CONV_WIDTH, CONV_CH), lambda b, i: (0, 0)),
            vec, vec, vec,
        ],
        out_specs=pl.BlockSpec((None, tm, CONV_CH), lambda b, i: (b, i, 0)),
        out_shape=jax.ShapeDtypeStruct((batch, seq, CONV_CH), BF16),
        scratch_shapes=[pltpu.VMEM((tm + 2 * HALO_ROWS, CONV_CH), F32)],
        compiler_params=_params("parallel", "parallel"),
        name="conv_module",
    )(glu, glu, glu, conv_dw_w, conv_dw_b, conv_ln_g, conv_ln_b)


def _merge_kernel(o_ref, c_ref, g_ref, wa_ref, wc_ref, m_ref):
    a = jnp.dot(o_ref[...], wa_ref[...], preferred_element_type=F32)
    cb = jnp.dot(c_ref[...], wc_ref[...], preferred_element_type=F32)
    m = g_ref[:, :D_MODEL].astype(F32) * a + g_ref[:, D_MODEL:].astype(F32) * cb
    m_ref[...] = m.astype(BF16)


def _merge(o2, c2, gates, w_attn_proj, w_conv_proj, tm):
    rows = o2.shape[0]
    return pl.pallas_call(
        _merge_kernel,
        grid=(rows // tm,),
        in_specs=[
            pl.BlockSpec((tm, V_WIDTH), lambda i: (i, 0)),
            pl.BlockSpec((tm, CONV_CH), lambda i: (i, 0)),
            pl.BlockSpec((tm, N_BRANCH * D_MODEL), lambda i: (i, 0)),
            _resident((V_WIDTH, D_MODEL)),
            _resident((CONV_CH, D_MODEL)),
        ],
        out_specs=pl.BlockSpec((tm, D_MODEL), lambda i: (i, 0)),
        out_shape=jax.ShapeDtypeStruct((rows, D_MODEL), BF16),
        compiler_params=_params("parallel"),
        name="merge",
    )(o2, c2, gates, w_attn_proj, w_conv_proj)


def _out_proj_kernel(x_ref, m_ref, w_ref, y_ref):
    y_ref[...] = x_ref[...] + jnp.dot(m_ref[...], w_ref[...], preferred_element_type=F32)


def _out_proj(x2, m2, w_out, tm):
    rows = x2.shape[0]
    return pl.pallas_call(
        _out_proj_kernel,
        grid=(rows // tm,),
        in_specs=[
            pl.BlockSpec((tm, D_MODEL), lambda i: (i, 0)),
            pl.BlockSpec((tm, D_MODEL), lambda i: (i, 0)),
            _resident((D_MODEL, D_MODEL)),
        ],
        out_specs=pl.BlockSpec((tm, D_MODEL), lambda i: (i, 0)),
        out_shape=jax.ShapeDtypeStruct((rows, D_MODEL), F32),
        compiler_params=_params("parallel"),
        name="out_proj",
    )(x2, m2, w_out)


def _ffn_up_kernel(x_ref, g_ref, w_ref, up_ref, h_ref):
    @pl.when(pl.program_id(1) == 0)
    def _():
        x = x_ref[...]
        ms = jnp.mean(x * x, axis=-1, keepdims=True)
        h_ref[...] = (x * lax.rsqrt(ms + EPS) * g_ref[...]).astype(BF16)

    up_ref[...] = jnp.dot(h_ref[...], w_ref[...], preferred_element_type=F32).astype(BF16)


def _ffn_up(x2, norm2_g, w_up, tm):
    rows = x2.shape[0]
    tn = 1408
    return pl.pallas_call(
        _ffn_up_kernel,
        grid=(rows // tm, 2 * FFN_DIM // tn),
        in_specs=[
            pl.BlockSpec((tm, D_MODEL), lambda i, j: (i, 0)),
            pl.BlockSpec((1, D_MODEL), lambda i, j: (0, 0)),
            pl.BlockSpec((D_MODEL, tn), lambda i, j: (0, j)),
        ],
        out_specs=pl.BlockSpec((tm, tn), lambda i, j: (i, j)),
        out_shape=jax.ShapeDtypeStruct((rows, 2 * FFN_DIM), BF16),
        scratch_shapes=[pltpu.VMEM((tm, D_MODEL), BF16)],
        compiler_params=_params("parallel", "arbitrary"),
        name="ffn_up",
    )(x2, norm2_g, w_up)


def _conv3(prev_ref, cur_ref, next_ref, w_ref, b_ref, first, last):
    cur = cur_ref[...].astype(F32)
    tm = cur.shape[0]
    row = lax.broadcasted_iota(jnp.int32, cur.shape, 0)
    before = jnp.where(first, 0.0, prev_ref[HALO_ROWS - 1:HALO_ROWS, :].astype(F32))
    after = jnp.where(last, 0.0, next_ref[0:1, :].astype(F32))
    up_m1 = jnp.where(row == 0, before, pltpu.roll(cur, 1, 0))
    up_p1 = jnp.where(row == tm - 1, after, pltpu.roll(cur, tm - 1, 0))
    return w_ref[0:1, :] * up_m1 + w_ref[1:2, :] * cur + w_ref[2:3, :] * up_p1 + b_ref[...]


def _ffn_down_kernel(pa_ref, a_ref, na_ref, pb_ref, b_ref, nb_ref, wa_ref, ba_ref, wb_ref, bb_ref,
                     wd_ref, x_ref, y_ref):
    i = pl.program_id(1)
    f = pl.program_id(2)
    first = i == 0
    last = i == pl.num_programs(1) - 1
    fa = _conv3(pa_ref, a_ref, na_ref, wa_ref, ba_ref, first, last)
    fb = _conv3(pb_ref, b_ref, nb_ref, wb_ref, bb_ref, first, last)
    act = (fa * jax.nn.sigmoid(fa) * fb).astype(BF16)
    contrib = jnp.dot(act, wd_ref[...], preferred_element_type=F32)

    @pl.when(f == 0)
    def _():
        y_ref[...] = x_ref[...] + contrib

    @pl.when(f > 0)
    def _():
        y_ref[...] += contrib


def _ffn_down(up, x1, ffn_dw_w, ffn_dw_b, w_down):
    batch, seq, _ = up.shape
    tm = _row_tile(seq)
    tf = 512
    nf = FFN_DIM // tf
    prev_a, next_a = _halo_specs(tm, seq, tf, lambda f: f)
    prev_b, next_b = _halo_specs(tm, seq, tf, lambda f: f + nf)
    return pl.pallas_call(
        _ffn_down_kernel,
        grid=(batch, seq // tm, nf),
        in_specs=[
            prev_a, pl.BlockSpec((None, tm, tf), lambda b, i, f: (b, i, f)), next_a,
            prev_b, pl.BlockSpec((None, tm, tf), lambda b, i, f: (b, i, f + nf)), next_b,
            pl.BlockSpec((FFN_CONV_WIDTH, tf), lambda b, i, f: (0, f)),
            pl.BlockSpec((1, tf), lambda b, i, f: (0, f)),
            pl.BlockSpec((FFN_CONV_WIDTH, tf), lambda b, i, f: (0, f + nf)),
            pl.BlockSpec((1, tf), lambda b, i, f: (0, f + nf)),
            pl.BlockSpec((tf, D_MODEL), lambda b, i, f: (f, 0)),
            pl.BlockSpec((None, tm, D_MODEL), lambda b, i, f: (b, i, 0)),
        ],
        out_specs=pl.BlockSpec((None, tm, D_MODEL), lambda b, i, f: (b, i, 0)),
        out_shape=jax.ShapeDtypeStruct((batch, seq, D_MODEL), F32),
        compiler_params=_params("parallel", "parallel", "arbitrary"),
        name="ffn_down",
    )(up, up, up, up, up, up, ffn_dw_w, ffn_dw_b, ffn_dw_w, ffn_dw_b, w_down, x1)


def _rope_tables(seq):
    inv_freq = 1.0 / (ROPE_THETA ** (jnp.arange(0, HEAD_DIM, 2, dtype=F32) / HEAD_DIM))
    ang = jnp.arange(seq, dtype=F32)[:, None] * inv_freq[None, :]
    ang = jnp.concatenate([ang, ang], axis=-1)
    sign = jnp.where(jnp.arange(HEAD_DIM) < HEAD_DIM // 2, -1.0, 1.0).astype(F32)
    return jnp.cos(ang), jnp.sin(ang) * sign


def _encoder_layer(x, layer_idx, p):
    batch, seq, _ = x.shape
    rows = batch * seq
    tm = _row_tile(seq)
    lam_init = 0.8 - 0.6 * math.exp(-0.3 * layer_idx)
    cos, sin_signed = _rope_tables(seq)
    x2 = x.reshape(rows, D_MODEL)

    q, k, v, glu, gates = _in_proj(x2, seq, p["norm1_g"], p["w_in"], p["b_gate"], p["q_norm_g"],
                                   p["k_norm_g"], cos, sin_signed)
    o = _attention(q.reshape(batch, seq, QK_WIDTH), k.reshape(batch, seq, QK_WIDTH),
                   v.reshape(batch, seq, V_WIDTH), p["lambda_q1"], p["lambda_k1"], p["lambda_q2"],
                   p["lambda_k2"], p["subln_g"], lam_init)
    c = _conv_module(glu.reshape(batch, seq, CONV_CH), p["conv_dw_w"], p["conv_dw_b"],
                     p["conv_ln_g"], p["conv_ln_b"])
    m = _merge(o.reshape(rows, V_WIDTH), c.reshape(rows, CONV_CH), gates, p["w_attn_proj"],
               p["w_conv_proj"], tm)
    x1 = _out_proj(x2, m, p["w_out"], tm)
    up = _ffn_up(x1, p["norm2_g"], p["w_up"], tm)
    y = _ffn_down(up.reshape(batch, seq, 2 * FFN_DIM), x1.reshape(batch, seq, D_MODEL),
                  p["ffn_dw_w"], p["ffn_dw_b"], p["w_down"])
    return y


_MATRICES = ("w_in", "w_attn_proj", "w_conv_proj", "w_out", "w_up", "w_down")
_ROW_VECTORS = ("norm1_g", "b_gate", "q_norm_g", "k_norm_g", "lambda_q1", "lambda_k1", "lambda_q2",
                "lambda_k2", "subln_g", "conv_dw_b", "conv_ln_g", "conv_ln_b", "norm2_g", "ffn_dw_b")


def kernel(x_prompt, x_sample, norm1_g, w_in, b_gate, q_norm_g, k_norm_g, lambda_q1, lambda_k1, lambda_q2, lambda_k2, subln_g, w_attn_proj, conv_dw_w, conv_dw_b, conv_ln_g, conv_ln_b, w_conv_proj, w_out, norm2_g, w_up, ffn_dw_w, ffn_dw_b, w_down):
    stacked = dict(norm1_g=norm1_g, w_in=w_in, b_gate=b_gate, q_norm_g=q_norm_g, k_norm_g=k_norm_g,
                   lambda_q1=lambda_q1, lambda_k1=lambda_k1, lambda_q2=lambda_q2,
                   lambda_k2=lambda_k2, subln_g=subln_g, w_attn_proj=w_attn_proj,
                   conv_dw_w=conv_dw_w, conv_dw_b=conv_dw_b, conv_ln_g=conv_ln_g,
                   conv_ln_b=conv_ln_b, w_conv_proj=w_conv_proj, w_out=w_out, norm2_g=norm2_g,
                   w_up=w_up, ffn_dw_w=ffn_dw_w, ffn_dw_b=ffn_dw_b, w_down=w_down)
    depth = w_in.shape[0]
    layers = []
    for l in range(depth):
        p = {}
        for name, value in stacked.items():
            value = value[l]
            if name in _MATRICES:
                value = value.astype(BF16)
            elif name in _ROW_VECTORS:
                value = value.astype(F32)[None, :]
            else:
                value = value.astype(F32)
            p[name] = value
        layers.append(p)

    def run(x):
        for l, p in enumerate(layers):
            x = _encoder_layer(x, l, p)
        return x

    return run(x_prompt), run(x_sample)
```

```python
import functools
import math

import jax
import jax.numpy as jnp
from jax import lax
from jax.experimental import pallas as pl
from jax.experimental.pallas import tpu as pltpu

F32 = jnp.float32
BF16 = jnp.bfloat16

D_MODEL = 2048
N_HEADS = 8
HEAD_DIM = 128
V_HEAD_DIM = 2 * HEAD_DIM
QK_WIDTH = 2 * N_HEADS * HEAD_DIM
V_WIDTH = N_HEADS * V_HEAD_DIM
CONV_CH = 1024
CONV_WIDTH = 31
N_BRANCH = 2
FFN_DIM = 5632
FFN_CONV_WIDTH = 3
ROPE_THETA = 10000.0
EPS = 1e-6
IN_WIDTH = 2 * QK_WIDTH + V_WIDTH + 2 * CONV_CH + N_BRANCH * D_MODEL

VMEM_LIMIT_BYTES = 56 * 1024 * 1024
HALO_ROWS = 16
IN_COL_TILE = 1024
LOG2E = math.log2(math.e)


def _params(*semantics):
    return pltpu.CompilerParams(dimension_semantics=semantics, vmem_limit_bytes=VMEM_LIMIT_BYTES)


def _resident(shape):
    return pl.BlockSpec(shape, lambda *_: (0,) * len(shape), pipeline_mode=pl.Buffered(1))


def _row_tile(seq):
    return min(512, seq)


def _qk_epilogue(z, gain, cos, sin_signed, out_scale):
    outs = []
    for h in range(z.shape[1] // HEAD_DIM):
        zh = z[:, h * HEAD_DIM:(h + 1) * HEAD_DIM]
        ms = jnp.mean(zh * zh, axis=-1, keepdims=True)
        y = zh * lax.rsqrt(ms + EPS) * gain
        y = y * cos + pltpu.roll(y, HEAD_DIM // 2, 1) * sin_signed
        outs.append((y * out_scale).astype(BF16))
    return jnp.concatenate(outs, axis=1)


def _in_proj_kernel(x_ref, g1_ref, w_ref, bg_ref, qg_ref, kg_ref, cos_ref, sin_ref,
                    q_ref, k_ref, v_ref, glu_ref, gate_ref, h_ref, ua_ref, *, q_scale):
    j = pl.program_id(1)

    @pl.when(j == 0)
    def _():
        x = x_ref[...]
        ms = jnp.mean(x * x, axis=-1, keepdims=True)
        h_ref[...] = (x * lax.rsqrt(ms + EPS) * g1_ref[...]).astype(BF16)

    z = jnp.dot(h_ref[...], w_ref[...], preferred_element_type=F32)

    @pl.when(j < 2)
    def _():
        q_ref[...] = _qk_epilogue(z, qg_ref[...], cos_ref[...], sin_ref[...], q_scale)

    @pl.when((j >= 2) & (j < 4))
    def _():
        k_ref[...] = _qk_epilogue(z, kg_ref[...], cos_ref[...], sin_ref[...], 1.0)

    @pl.when((j >= 4) & (j < 6))
    def _():
        v_ref[...] = z.astype(BF16)

    @pl.when(j == 6)
    def _():
        ua_ref[...] = z

    @pl.when(j == 7)
    def _():
        glu_ref[...] = (ua_ref[...] * jax.nn.sigmoid(z)).astype(BF16)

    @pl.when(j >= 8)
    def _():
        gate_ref[...] = jax.nn.sigmoid(z + bg_ref[...]).astype(BF16)


def _in_proj(x2, seq, norm1_g, w_in, b_gate, q_norm_g, k_norm_g, cos, sin_signed):
    rows = x2.shape[0]
    tm = _row_tile(seq)
    tn = IN_COL_TILE
    seq_tiles = seq // tm

    def clamp(j, lo, n):
        return jnp.clip(j - lo, 0, n - 1)

    kernel = functools.partial(_in_proj_kernel, q_scale=HEAD_DIM ** -0.5 * LOG2E)
    out_shapes = (
        jax.ShapeDtypeStruct((rows, QK_WIDTH), BF16),
        jax.ShapeDtypeStruct((rows, QK_WIDTH), BF16),
        jax.ShapeDtypeStruct((rows, V_WIDTH), BF16),
        jax.ShapeDtypeStruct((rows, CONV_CH), BF16),
        jax.ShapeDtypeStruct((rows, N_BRANCH * D_MODEL), BF16),
    )
    return pl.pallas_call(
        kernel,
        grid=(rows // tm, IN_WIDTH // tn),
        in_specs=[
            pl.BlockSpec((tm, D_MODEL), lambda i, j: (i, 0)),
            pl.BlockSpec((1, D_MODEL), lambda i, j: (0, 0)),
            pl.BlockSpec((D_MODEL, tn), lambda i, j: (0, j)),
            pl.BlockSpec((1, tn), lambda i, j: (0, clamp(j, 8, 4))),
            pl.BlockSpec((1, HEAD_DIM), lambda i, j: (0, 0)),
            pl.BlockSpec((1, HEAD_DIM), lambda i, j: (0, 0)),
            pl.BlockSpec((tm, HEAD_DIM), lambda i, j: (i % seq_tiles, 0)),
            pl.BlockSpec((tm, HEAD_DIM), lambda i, j: (i % seq_tiles, 0)),
        ],
        out_specs=(
            pl.BlockSpec((tm, tn), lambda i, j: (i, clamp(j, 0, 2))),
            pl.BlockSpec((tm, tn), lambda i, j: (i, clamp(j, 2, 2))),
            pl.BlockSpec((tm, tn), lambda i, j: (i, clamp(j, 4, 2))),
            pl.BlockSpec((tm, CONV_CH), lambda i, j: (i, 0)),
            pl.BlockSpec((tm, tn), lambda i, j: (i, clamp(j, 8, 4))),
        ),
        out_shape=out_shapes,
        scratch_shapes=[pltpu.VMEM((tm, D_MODEL), BF16), pltpu.VMEM((tm, CONV_CH), F32)],
        compiler_params=_params("parallel", "arbitrary"),
        name="in_proj",
    )(x2, norm1_g, w_in, b_gate, q_norm_g, k_norm_g, cos, sin_signed)


def _attn_kernel(lq1_ref, lk1_ref, lq2_ref, lk2_ref, sg_ref, q_ref, k_ref, v_ref, o_ref,
                 s0_ref, s1_ref, p0_ref, p1_ref, a0_ref, a1_ref, m_ref, l_ref, acc_ref,
                 *, k_chunk, lam_init):
    tq = q_ref.shape[0]
    n_chunks = k_ref.shape[0] // k_chunk
    s_bufs, p_bufs, a_bufs = (s0_ref, s1_ref), (p0_ref, p1_ref), (a0_ref, a1_ref)
    lane_tiles = k_chunk // HEAD_DIM

    def scores(c, slot):
        c0 = pl.multiple_of(c * k_chunk, k_chunk)
        for sub in range(2):
            lanes = slice(sub * HEAD_DIM, (sub + 1) * HEAD_DIM)
            s_bufs[slot][sub * tq:(sub + 1) * tq, :] = lax.dot_general(
                q_ref[:, lanes], k_ref[pl.ds(c0, k_chunk), lanes],
                (((1,), (1,)), ((), ())), preferred_element_type=F32)

    def softmax(slot):
        s = s_bufs[slot][...]
        m_old = m_ref[...]
        m_new = jnp.maximum(m_old, jnp.max(s, axis=-1, keepdims=True))
        p = jnp.exp2(s - pltpu.repeat(m_new, lane_tiles, 1))
        alpha = jnp.exp2(m_old - m_new)
        p_sum = p[:, :HEAD_DIM]
        for t in range(1, lane_tiles):
            p_sum = p_sum + p[:, t * HEAD_DIM:(t + 1) * HEAD_DIM]
        l_ref[...] = alpha * l_ref[...] + p_sum
        m_ref[...] = m_new
        a_bufs[slot][...] = alpha
        p_bufs[slot][...] = p.astype(BF16)

    def values(c, slot):
        c0 = pl.multiple_of(c * k_chunk, k_chunk)
        pv = jnp.dot(p_bufs[slot][...], v_ref[pl.ds(c0, k_chunk), :], preferred_element_type=F32)
        acc_ref[...] = pltpu.repeat(a_bufs[slot][...], V_HEAD_DIM // HEAD_DIM, 1) * acc_ref[...] + pv

    m_ref[...] = jnp.full(m_ref.shape, -jnp.inf, F32)
    l_ref[...] = jnp.zeros(l_ref.shape, F32)
    acc_ref[...] = jnp.zeros(acc_ref.shape, F32)

    scores(0, 0)
    scores(1, 1)
    softmax(0)

    def body(t, carry):
        c = 2 * t + 2
        scores(c, 0)
        softmax(1)
        values(c - 2, 0)
        scores(c + 1, 1)
        softmax(0)
        values(c - 1, 1)
        return carry

    lax.fori_loop(0, n_chunks // 2 - 1, body, 0)
    softmax(1)
    values(n_chunks - 2, 0)
    values(n_chunks - 1, 1)

    lam = (jnp.exp(jnp.sum(lq1_ref[...] * lk1_ref[...], axis=-1, keepdims=True))
           - jnp.exp(jnp.sum(lq2_ref[...] * lk2_ref[...], axis=-1, keepdims=True)) + lam_init)
    o = acc_ref[...] / jnp.sum(l_ref[...], axis=-1, keepdims=True)
    o = o[:tq] - lam * o[tq:]
    ms = jnp.mean(o * o, axis=-1, keepdims=True)
    o = o * lax.rsqrt(ms + EPS) * sg_ref[...] * (1.0 - lam_init)
    o_ref[...] = o.astype(BF16)


def _attention(q, k, v, lq1, lk1, lq2, lk2, subln_g, lam_init):
    batch, seq, _ = q.shape
    tq = min(512, seq)
    k_chunk = min(512, seq // 2)
    assert (seq // k_chunk) % 2 == 0
    vec = pl.BlockSpec((1, HEAD_DIM), lambda b, h, i: (0, 0))
    kernel = functools.partial(_attn_kernel, k_chunk=k_chunk, lam_init=lam_init)
    s_buf = pltpu.VMEM((2 * tq, k_chunk), F32)
    p_buf = pltpu.VMEM((2 * tq, k_chunk), BF16)
    stat = pltpu.VMEM((2 * tq, HEAD_DIM), F32)
    return pl.pallas_call(
        kernel,
        grid=(batch, N_HEADS, seq // tq),
        in_specs=[
            vec, vec, vec, vec,
            pl.BlockSpec((1, V_HEAD_DIM), lambda b, h, i: (0, 0)),
            pl.BlockSpec((None, tq, V_HEAD_DIM), lambda b, h, i: (b, i, h)),
            pl.BlockSpec((None, seq, V_HEAD_DIM), lambda b, h, i: (b, 0, h)),
            pl.BlockSpec((None, seq, V_HEAD_DIM), lambda b, h, i: (b, 0, h)),
        ],
        out_specs=pl.BlockSpec((None, tq, V_HEAD_DIM), lambda b, h, i: (b, i, h)),
        out_shape=jax.ShapeDtypeStruct((batch, seq, V_WIDTH), BF16),
        scratch_shapes=[s_buf, s_buf, p_buf, p_buf, stat, stat, stat, stat,
                        pltpu.VMEM((2 * tq, V_HEAD_DIM), F32)],
        compiler_params=_params("parallel", "parallel", "arbitrary"),
        name="diff_attention",
    )(lq1, lk1, lq2, lk2, subln_g, q, k, v)


def _conv_kernel(prev_ref, cur_ref, next_ref, w_ref, b_ref, lg_ref, lb_ref, o_ref, ext_ref,
                 *, row_chunk):
    i = pl.program_id(1)
    tm = cur_ref.shape[0]
    pad = (CONV_WIDTH - 1) // 2
    ext_ref[0:HALO_ROWS] = jnp.where(i > 0, prev_ref[...].astype(F32), 0.0)
    ext_ref[HALO_ROWS:HALO_ROWS + tm] = cur_ref[...].astype(F32)
    ext_ref[HALO_ROWS + tm:] = jnp.where(i < pl.num_programs(1) - 1, next_ref[...].astype(F32), 0.0)

    def body(r, carry):
        r0 = pl.multiple_of(r * row_chunk, row_chunk)
        win = ext_ref[pl.ds(r0, row_chunk + 2 * HALO_ROWS), :]
        acc = jnp.broadcast_to(b_ref[...], (row_chunk, CONV_CH))
        for t in range(CONV_WIDTH):
            off = HALO_ROWS - pad + t
            acc = acc + w_ref[t:t + 1, :] * win[off:off + row_chunk, :]
        mu = jnp.mean(acc, axis=-1, keepdims=True)
        xc = acc - mu
        var = jnp.mean(xc * xc, axis=-1, keepdims=True)
        y = xc * lax.rsqrt(var + EPS) * lg_ref[...] + lb_ref[...]
        o_ref[pl.ds(r0, row_chunk), :] = (y * jax.nn.sigmoid(y)).astype(BF16)
        return carry

    lax.fori_loop(0, tm // row_chunk, body, 0)


def _halo_specs(tm, seq, width, col_map):
    per_tile = tm // HALO_ROWS
    last = seq // HALO_ROWS - 1
    prev = pl.BlockSpec((None, HALO_ROWS, width),
                        lambda b, i, *f: (b, jnp.maximum(i * per_tile - 1, 0), col_map(*f)))
    nxt = pl.BlockSpec((None, HALO_ROWS, width),
                       lambda b, i, *f: (b, jnp.minimum((i + 1) * per_tile, last), col_map(*f)))
    return prev, nxt


def _conv_module(glu, conv_dw_w, conv_dw_b, conv_ln_g, conv_ln_b):
    batch, seq, _ = glu.shape
    tm = _row_tile(seq)
    prev, nxt = _halo_specs(tm, seq, CONV_CH, lambda: 0)
    vec = pl.BlockSpec((1, CONV_CH), lambda b, i: (0, 0))
    return pl.pallas_call(
        functools.partial(_conv_kernel, row_chunk=16),
        grid=(batch, seq // tm),
        in_specs=[
            prev,
            pl.BlockSpec((None, tm, CONV_CH), lambda b, i: (b, i, 0)),
            nxt,
            pl.BlockSpec((CONV_WIDTH, CONV_CH), lambda b, i: (0, 0)),
            vec, vec, vec,
        ],
        out_specs=pl.BlockSpec((None, tm, CONV_CH), lambda b, i: (b, i, 0)),
        out_shape=jax.ShapeDtypeStruct((batch, seq, CONV_CH), BF16),
        scratch_shapes=[pltpu.VMEM((tm + 2 * HALO_ROWS, CONV_CH), F32)],
        compiler_params=_params("parallel", "parallel"),
        name="conv_module",
    )(glu, glu, glu, conv_dw_w, conv_dw_b, conv_ln_g, conv_ln_b)


def _merge_kernel(o_ref, c_ref, g_ref, wa_ref, wc_ref, m_ref):
    a = jnp.dot(o_ref[...], wa_ref[...], preferred_element_type=F32)
    cb = jnp.dot(c_ref[...], wc_ref[...], preferred_element_type=F32)
    m = g_ref[:, :D_MODEL].astype(F32) * a + g_ref[:, D_MODEL:].astype(F32) * cb
    m_ref[...] = m.astype(BF16)


def _merge(o2, c2, gates, w_attn_proj, w_conv_proj, tm):
    rows = o2.shape[0]
    return pl.pallas_call(
        _merge_kernel,
        grid=(rows // tm,),
        in_specs=[
            pl.BlockSpec((tm, V_WIDTH), lambda i: (i, 0)),
            pl.BlockSpec((tm, CONV_CH), lambda i: (i, 0)),
            pl.BlockSpec((tm, N_BRANCH * D_MODEL), lambda i: (i, 0)),
            _resident((V_WIDTH, D_MODEL)),
            _resident((CONV_CH, D_MODEL)),
        ],
        out_specs=pl.BlockSpec((tm, D_MODEL), lambda i: (i, 0)),
        out_shape=jax.ShapeDtypeStruct((rows, D_MODEL), BF16),
        compiler_params=_params("parallel"),
        name="merge",
    )(o2, c2, gates, w_attn_proj, w_conv_proj)


def _out_proj_kernel(x_ref, m_ref, w_ref, y_ref):
    y_ref[...] = x_ref[...] + jnp.dot(m_ref[...], w_ref[...], preferred_element_type=F32)


def _out_proj(x2, m2, w_out, tm):
    rows = x2.shape[0]
    return pl.pallas_call(
        _out_proj_kernel,
        grid=(rows // tm,),
        in_specs=[
            pl.BlockSpec((tm, D_MODEL), lambda i: (i, 0)),
            pl.BlockSpec((tm, D_MODEL), lambda i: (i, 0)),
            _resident((D_MODEL, D_MODEL)),
        ],
        out_specs=pl.BlockSpec((tm, D_MODEL), lambda i: (i, 0)),
        out_shape=jax.ShapeDtypeStruct((rows, D_MODEL), F32),
        compiler_params=_params("parallel"),
        name="out_proj",
    )(x2, m2, w_out)


def _ffn_up_kernel(x_ref, g_ref, w_ref, up_ref, h_ref):
    @pl.when(pl.program_id(1) == 0)
    def _():
        x = x_ref[...]
        ms = jnp.mean(x * x, axis=-1, keepdims=True)
        h_ref[...] = (x * lax.rsqrt(ms + EPS) * g_ref[...]).astype(BF16)

    up_ref[...] = jnp.dot(h_ref[...], w_ref[...], preferred_element_type=F32).astype(BF16)


def _ffn_up(x2, norm2_g, w_up, tm):
    rows = x2.shape[0]
    tn = 1408
    return pl.pallas_call(
        _ffn_up_kernel,
        grid=(rows // tm, 2 * FFN_DIM // tn),
        in_specs=[
            pl.BlockSpec((tm, D_MODEL), lambda i, j: (i, 0)),
            pl.BlockSpec((1, D_MODEL), lambda i, j: (0, 0)),
            pl.BlockSpec((D_MODEL, tn), lambda i, j: (0, j)),
        ],
        out_specs=pl.BlockSpec((tm, tn), lambda i, j: (i, j)),
        out_shape=jax.ShapeDtypeStruct((rows, 2 * FFN_DIM), BF16),
        scratch_shapes=[pltpu.VMEM((tm, D_MODEL), BF16)],
        compiler_params=_params("parallel", "arbitrary"),
        name="ffn_up",
    )(x2, norm2_g, w_up)


def _conv3(prev_ref, cur_ref, next_ref, w_ref, b_ref, first, last):
    cur = cur_ref[...].astype(F32)
    tm = cur.shape[0]
    row = lax.broadcasted_iota(jnp.int32, cur.shape, 0)
    before = jnp.where(first, 0.0, prev_ref[HALO_ROWS - 1:HALO_ROWS, :].astype(F32))
    after = jnp.where(last, 0.0, next_ref[0:1, :].astype(F32))
    up_m1 = jnp.where(row == 0, before, pltpu.roll(cur, 1, 0))
    up_p1 = jnp.where(row == tm - 1, after, pltpu.roll(cur, tm - 1, 0))
    return w_ref[0:1, :] * up_m1 + w_ref[1:2, :] * cur + w_ref[2:3, :] * up_p1 + b_ref[...]


def _ffn_down_kernel(pa_ref, a_ref, na_ref, pb_ref, b_ref, nb_ref, wa_ref, ba_ref, wb_ref, bb_ref,
                     wd_ref, x_ref, y_ref):
    i = pl.program_id(1)
    f = pl.program_id(2)
    first = i == 0
    last = i == pl.num_programs(1) - 1
    fa = _conv3(pa_ref, a_ref, na_ref, wa_ref, ba_ref, first, last)
    fb = _conv3(pb_ref, b_ref, nb_ref, wb_ref, bb_ref, first, last)
    act = (fa * jax.nn.sigmoid(fa) * fb).astype(BF16)
    contrib = jnp.dot(act, wd_ref[...], preferred_element_type=F32)

    @pl.when(f == 0)
    def _():
        y_ref[...] = x_ref[...] + contrib

    @pl.when(f > 0)
    def _():
        y_ref[...] += contrib


def _ffn_down(up, x1, ffn_dw_w, ffn_dw_b, w_down):
    batch, seq, _ = up.shape
    tm = _row_tile(seq)
    tf = 512
    nf = FFN_DIM // tf
    prev_a, next_a = _halo_specs(tm, seq, tf, lambda f: f)
    prev_b, next_b = _halo_specs(tm, seq, tf, lambda f: f + nf)
    return pl.pallas_call(
        _ffn_down_kernel,
        grid=(batch, seq // tm, nf),
        in_specs=[
            prev_a, pl.BlockSpec((None, tm, tf), lambda b, i, f: (b, i, f)), next_a,
            prev_b, pl.BlockSpec((None, tm, tf), lambda b, i, f: (b, i, f + nf)), next_b,
            pl.BlockSpec((FFN_CONV_WIDTH, tf), lambda b, i, f: (0, f)),
            pl.BlockSpec((1, tf), lambda b, i, f: (0, f)),
            pl.BlockSpec((FFN_CONV_WIDTH, tf), lambda b, i, f: (0, f + nf)),
            pl.BlockSpec((1, tf), lambda b, i, f: (0, f + nf)),
            pl.BlockSpec((tf, D_MODEL), lambda b, i, f: (f, 0)),
            pl.BlockSpec((None, tm, D_MODEL), lambda b, i, f: (b, i, 0)),
        ],
        out_specs=pl.BlockSpec((None, tm, D_MODEL), lambda b, i, f: (b, i, 0)),
        out_shape=jax.ShapeDtypeStruct((batch, seq, D_MODEL), F32),
        compiler_params=_params("parallel", "parallel", "arbitrary"),
        name="ffn_down",
    )(up, up, up, up, up, up, ffn_dw_w, ffn_dw_b, ffn_dw_w, ffn_dw_b, w_down, x1)


def _rope_tables(seq):
    inv_freq = 1.0 / (ROPE_THETA ** (jnp.arange(0, HEAD_DIM, 2, dtype=F32) / HEAD_DIM))
    ang = jnp.arange(seq, dtype=F32)[:, None] * inv_freq[None, :]
    ang = jnp.concatenate([ang, ang], axis=-1)
    sign = jnp.where(jnp.arange(HEAD_DIM) < HEAD_DIM // 2, -1.0, 1.0).astype(F32)
    return jnp.cos(ang), jnp.sin(ang) * sign


def _encoder_layer(x, layer_idx, p):
    batch, seq, _ = x.shape
    rows = batch * seq
    tm = _row_tile(seq)
    lam_init = 0.8 - 0.6 * math.exp(-0.3 * layer_idx)
    cos, sin_signed = _rope_tables(seq)
    x2 = x.reshape(rows, D_MODEL)

    q, k, v, glu, gates = _in_proj(x2, seq, p["norm1_g"], p["w_in"], p["b_gate"], p["q_norm_g"],
                                   p["k_norm_g"], cos, sin_signed)
    o = _attention(q.reshape(batch, seq, QK_WIDTH), k.reshape(batch, seq, QK_WIDTH),
                   v.reshape(batch, seq, V_WIDTH), p["lambda_q1"], p["lambda_k1"], p["lambda_q2"],
                   p["lambda_k2"], p["subln_g"], lam_init)
    c = _conv_module(glu.reshape(batch, seq, CONV_CH), p["conv_dw_w"], p["conv_dw_b"],
                     p["conv_ln_g"], p["conv_ln_b"])
    m = _merge(o.reshape(rows, V_WIDTH), c.reshape(rows, CONV_CH), gates, p["w_attn_proj"],
               p["w_conv_proj"], tm)
    x1 = _out_proj(x2, m, p["w_out"], tm)
    up = _ffn_up(x1, p["norm2_g"], p["w_up"], tm)
    y = _ffn_down(up.reshape(batch, seq, 2 * FFN_DIM), x1.reshape(batch, seq, D_MODEL),
                  p["ffn_dw_w"], p["ffn_dw_b"], p["w_down"])
    return y


_MATRICES = ("w_in", "w_attn_proj", "w_conv_proj", "w_out", "w_up", "w_down")
_ROW_VECTORS = ("norm1_g", "b_gate", "q_norm_g", "k_norm_g", "lambda_q1", "lambda_k1", "lambda_q2",
                "lambda_k2", "subln_g", "conv_dw_b", "conv_ln_g", "conv_ln_b", "norm2_g", "ffn_dw_b")


def kernel(x_prompt, x_sample, norm1_g, w_in, b_gate, q_norm_g, k_norm_g, lambda_q1, lambda_k1, lambda_q2, lambda_k2, subln_g, w_attn_proj, conv_dw_w, conv_dw_b, conv_ln_g, conv_ln_b, w_conv_proj, w_out, norm2_g, w_up, ffn_dw_w, ffn_dw_b, w_down):
    stacked = dict(norm1_g=norm1_g, w_in=w_in, b_gate=b_gate, q_norm_g=q_norm_g, k_norm_g=k_norm_g,
                   lambda_q1=lambda_q1, lambda_k1=lambda_k1, lambda_q2=lambda_q2,
                   lambda_k2=lambda_k2, subln_g=subln_g, w_attn_proj=w_attn_proj,
                   conv_dw_w=conv_dw_w, conv_dw_b=conv_dw_b, conv_ln_g=conv_ln_g,
                   conv_ln_b=conv_ln_b, w_conv_proj=w_conv_proj, w_out=w_out, norm2_g=norm2_g,
                   w_up=w_up, ffn_dw_w=ffn_dw_w, ffn_dw_b=ffn_dw_b, w_down=w_down)
    depth = w_in.shape[0]
    layers = []
    for l in range(depth):
        p = {}
        for name, value in stacked.items():
            value = value[l]
            if name in _MATRICES:
                value = value.astype(BF16)
            elif name in _ROW_VECTORS:
                value = value.astype(F32)[None, :]
            else:
                value = value.astype(F32)
            p[name] = value
        layers.append(p)

    def run(x):
        for l, p in enumerate(layers):
            x = _encoder_layer(x, l, p)
        return x

    return run(x_prompt), run(x_sample)
```

```python
import functools
import math

import jax
import jax.numpy as jnp
from jax import lax
from jax.experimental import pallas as pl
from jax.experimental.pallas import tpu as pltpu

F32 = jnp.float32
BF16 = jnp.bfloat16

D_MODEL = 2048
N_HEADS = 8
HEAD_DIM = 128
V_HEAD_DIM = 2 * HEAD_DIM
QK_WIDTH = 2 * N_HEADS * HEAD_DIM
V_WIDTH = N_HEADS * V_HEAD_DIM
CONV_CH = 1024
CONV_WIDTH = 31
N_BRANCH = 2
FFN_DIM = 5632
FFN_CONV_WIDTH = 3
ROPE_THETA = 10000.0
EPS = 1e-6
IN_WIDTH = 2 * QK_WIDTH + V_WIDTH + 2 * CONV_CH + N_BRANCH * D_MODEL

VMEM_LIMIT_BYTES = 56 * 1024 * 1024
HALO_ROWS = 16
IN_COL_TILE = 1024
SCORE_COLS = 512
LOG2E = math.log2(math.e)


def _params(*semantics):
    return pltpu.CompilerParams(dimension_semantics=semantics, vmem_limit_bytes=VMEM_LIMIT_BYTES)


def _resident(shape):
    return pl.BlockSpec(shape, lambda *_: (0,) * len(shape), pipeline_mode=pl.Buffered(1))


def _row_tile(seq):
    return min(512, seq)


def _qk_epilogue(z, gain, cos, sin_signed, out_scale):
    outs = []
    for h in range(z.shape[1] // HEAD_DIM):
        zh = z[:, h * HEAD_DIM:(h + 1) * HEAD_DIM]
        ms = jnp.mean(zh * zh, axis=-1, keepdims=True)
        y = zh * lax.rsqrt(ms + EPS) * gain
        y = y * cos + pltpu.roll(y, HEAD_DIM // 2, 1) * sin_signed
        outs.append((y * out_scale).astype(BF16))
    return jnp.concatenate(outs, axis=1)


def _in_proj_kernel(x_ref, g1_ref, w_ref, bg_ref, qg_ref, kg_ref, cos_ref, sin_ref,
                    q_ref, k_ref, v_ref, glu_ref, gate_ref, h_ref, ua_ref, *, q_scale):
    j = pl.program_id(1)

    @pl.when(j == 0)
    def _():
        x = x_ref[...]
        ms = jnp.mean(x * x, axis=-1, keepdims=True)
        h_ref[...] = (x * lax.rsqrt(ms + EPS) * g1_ref[...]).astype(BF16)

    z = jnp.dot(h_ref[...], w_ref[...], preferred_element_type=F32)

    @pl.when(j < 2)
    def _():
        q_ref[...] = _qk_epilogue(z, qg_ref[...], cos_ref[...], sin_ref[...], q_scale)

    @pl.when((j >= 2) & (j < 4))
    def _():
        k_ref[...] = _qk_epilogue(z, kg_ref[...], cos_ref[...], sin_ref[...], 1.0)

    @pl.when((j >= 4) & (j < 6))
    def _():
        v_ref[...] = z.astype(BF16)

    @pl.when(j == 6)
    def _():
        ua_ref[...] = z

    @pl.when(j == 7)
    def _():
        glu_ref[...] = ua_ref[...] * jax.nn.sigmoid(z)

    @pl.when(j >= 8)
    def _():
        gate_ref[...] = jax.nn.sigmoid(z + bg_ref[...]).astype(BF16)


def _in_proj(x2, seq, norm1_g, w_in, b_gate, q_norm_g, k_norm_g, cos, sin_signed):
    rows = x2.shape[0]
    tm = _row_tile(seq)
    tn = IN_COL_TILE
    seq_tiles = seq // tm

    def clamp(j, lo, n):
        return jnp.clip(j - lo, 0, n - 1)

    kernel = functools.partial(_in_proj_kernel, q_scale=HEAD_DIM ** -0.5 * LOG2E)
    out_shapes = (
        jax.ShapeDtypeStruct((rows, QK_WIDTH), BF16),
        jax.ShapeDtypeStruct((rows, QK_WIDTH), BF16),
        jax.ShapeDtypeStruct((rows, V_WIDTH), BF16),
        jax.ShapeDtypeStruct((rows, CONV_CH), F32),
        jax.ShapeDtypeStruct((rows, N_BRANCH * D_MODEL), BF16),
    )
    return pl.pallas_call(
        kernel,
        grid=(rows // tm, IN_WIDTH // tn),
        in_specs=[
            pl.BlockSpec((tm, D_MODEL), lambda i, j: (i, 0)),
            pl.BlockSpec((1, D_MODEL), lambda i, j: (0, 0)),
            pl.BlockSpec((D_MODEL, tn), lambda i, j: (0, j)),
            pl.BlockSpec((1, tn), lambda i, j: (0, clamp(j, 8, 4))),
            pl.BlockSpec((1, HEAD_DIM), lambda i, j: (0, 0)),
            pl.BlockSpec((1, HEAD_DIM), lambda i, j: (0, 0)),
            pl.BlockSpec((tm, HEAD_DIM), lambda i, j: (i % seq_tiles, 0)),
            pl.BlockSpec((tm, HEAD_DIM), lambda i, j: (i % seq_tiles, 0)),
        ],
        out_specs=(
            pl.BlockSpec((tm, tn), lambda i, j: (i, clamp(j, 0, 2))),
            pl.BlockSpec((tm, tn), lambda i, j: (i, clamp(j, 2, 2))),
            pl.BlockSpec((tm, tn), lambda i, j: (i, clamp(j, 4, 2))),
            pl.BlockSpec((tm, CONV_CH), lambda i, j: (i, 0)),
            pl.BlockSpec((tm, tn), lambda i, j: (i, clamp(j, 8, 4))),
        ),
        out_shape=out_shapes,
        scratch_shapes=[pltpu.VMEM((tm, D_MODEL), BF16), pltpu.VMEM((tm, CONV_CH), F32)],
        compiler_params=_params("parallel", "arbitrary"),
        name="in_proj",
    )(x2, norm1_g, w_in, b_gate, q_norm_g, k_norm_g, cos, sin_signed)


def _attn_kernel(lq1_ref, lk1_ref, lq2_ref, lk2_ref, sg_ref, q_ref, k_ref, v_ref, o_ref,
                 s0_ref, s1_ref, x0_ref, x1_ref, p0_ref, p1_ref, a0_ref, a1_ref, l0_ref, l1_ref,
                 m_ref, acc_ref,
                 *, tq, k_chunk, lam_init):
    n_chunks = k_ref.shape[0] // k_chunk
    n_units = (q_ref.shape[0] // tq) * n_chunks
    s_bufs, p_bufs, max_bufs = (s0_ref, s1_ref), (p0_ref, p1_ref), (x0_ref, x1_ref)
    a_bufs, l_bufs = (a0_ref, a1_ref), (l0_ref, l1_ref)
    lane_tiles = k_chunk // HEAD_DIM

    def lanes_repeat(x, n):
        return jnp.concatenate([x] * n, axis=1)

    def scores(u, slot):
        q0 = pl.multiple_of((u // n_chunks) * tq, tq)
        c0 = pl.multiple_of((u % n_chunks) * k_chunk, k_chunk)
        for sub in range(2):
            lanes = slice(sub * HEAD_DIM, (sub + 1) * HEAD_DIM)
            rows = slice(sub * tq, (sub + 1) * tq)
            q = q_ref[pl.ds(q0, tq), lanes]
            for part in range(k_chunk // SCORE_COLS):
                s = lax.dot_general(q, k_ref[pl.ds(c0 + part * SCORE_COLS, SCORE_COLS), lanes],
                                    (((1,), (1,)), ((), ())), preferred_element_type=F32)
                s_bufs[slot][rows, part * SCORE_COLS:(part + 1) * SCORE_COLS] = s
                part_max = s[:, :HEAD_DIM]
                for t in range(1, SCORE_COLS // HEAD_DIM):
                    part_max = jnp.maximum(part_max, s[:, t * HEAD_DIM:(t + 1) * HEAD_DIM])
                max_bufs[slot][rows, part * HEAD_DIM:(part + 1) * HEAD_DIM] = part_max

    def softmax(u, slot):
        m_old = m_ref[...]
        if slot == 0:
            m_old = m_old + jnp.where(u % n_chunks == 0, -jnp.inf, 0.0)
        m_new = jnp.maximum(m_old, jnp.max(max_bufs[slot][...], axis=-1, keepdims=True))
        p = jnp.exp2(s_bufs[slot][...] - lanes_repeat(m_new, lane_tiles))
        alpha = jnp.exp2(m_old - m_new)
        p_sum = p[:, :HEAD_DIM]
        for t in range(1, lane_tiles):
            p_sum = p_sum + p[:, t * HEAD_DIM:(t + 1) * HEAD_DIM]
        l_bufs[slot][...] = alpha * l_bufs[1 - slot][...] + p_sum
        m_ref[...] = m_new
        a_bufs[slot][...] = alpha
        p_bufs[slot][...] = p.astype(BF16)

    def values(u, slot):
        c0 = pl.multiple_of((u % n_chunks) * k_chunk, k_chunk)
        pv = jnp.dot(p_bufs[slot][...], v_ref[pl.ds(c0, k_chunk), :], preferred_element_type=F32)
        acc_ref[...] = lanes_repeat(a_bufs[slot][...], V_HEAD_DIM // HEAD_DIM) * acc_ref[...] + pv

    def normalise(tile):
        lam = (jnp.exp(jnp.sum(lq1_ref[...] * lk1_ref[...], axis=-1, keepdims=True))
               - jnp.exp(jnp.sum(lq2_ref[...] * lk2_ref[...], axis=-1, keepdims=True)) + lam_init)
        o = acc_ref[...] / jnp.sum(l1_ref[...], axis=-1, keepdims=True)
        o = o[:tq] - lam * o[tq:]
        ms = jnp.mean(o * o, axis=-1, keepdims=True)
        o = o * lax.rsqrt(ms + EPS) * sg_ref[...] * (1.0 - lam_init)
        o_ref[pl.ds(pl.multiple_of(tile * tq, tq), tq), :] = o.astype(BF16)

    m_ref[...] = jnp.full(m_ref.shape, -jnp.inf, F32)
    l1_ref[...] = jnp.zeros(l1_ref.shape, F32)
    acc_ref[...] = jnp.zeros(acc_ref.shape, F32)

    scores(0, 0)
    scores(1, 1)
    softmax(0, 0)

    def body(t, carry):
        u = 2 * t + 2
        scores(u, 0)
        softmax(u - 1, 1)
        values(u - 2, 0)
        scores(u + 1, 1)
        softmax(u, 0)
        values(u - 1, 1)

        @pl.when(u % n_chunks == 0)
        def _():
            normalise(u // n_chunks - 1)

        return carry

    lax.fori_loop(0, n_units // 2 - 1, body, 0)
    softmax(n_units - 1, 1)
    values(n_units - 2, 0)
    values(n_units - 1, 1)
    normalise(n_units // n_chunks - 1)


def _attention(q, k, v, lq1, lk1, lq2, lk2, subln_g, lam_init):
    batch, seq, _ = q.shape
    tq = min(512, seq)
    tq_group = min(2048, seq)
    k_chunk = min(1024, seq // 2)
    assert (seq // k_chunk) % 2 == 0
    vec = pl.BlockSpec((1, HEAD_DIM), lambda b, h, i: (0, 0))
    kernel = functools.partial(_attn_kernel, tq=tq, k_chunk=k_chunk, lam_init=lam_init)
    s_buf = pltpu.VMEM((2 * tq, k_chunk), F32)
    p_buf = pltpu.VMEM((2 * tq, k_chunk), BF16)
    stat = pltpu.VMEM((2 * tq, HEAD_DIM), F32)
    max_buf = pltpu.VMEM((2 * tq, k_chunk // SCORE_COLS * HEAD_DIM), F32)
    return pl.pallas_call(
        kernel,
        grid=(batch, N_HEADS, seq // tq_group),
        in_specs=[
            vec, vec, vec, vec,
            pl.BlockSpec((1, V_HEAD_DIM), lambda b, h, i: (0, 0)),
            pl.BlockSpec((None, tq_group, V_HEAD_DIM), lambda b, h, i: (b, i, h)),
            pl.BlockSpec((None, seq, V_HEAD_DIM), lambda b, h, i: (b, 0, h),
                         pipeline_mode=pl.Buffered(1)),
            pl.BlockSpec((None, seq, V_HEAD_DIM), lambda b, h, i: (b, 0, h),
                         pipeline_mode=pl.Buffered(1)),
        ],
        out_specs=pl.BlockSpec((None, tq_group, V_HEAD_DIM), lambda b, h, i: (b, i, h)),
        out_shape=jax.ShapeDtypeStruct((batch, seq, V_WIDTH), BF16),
        scratch_shapes=[s_buf, s_buf, max_buf, max_buf, p_buf, p_buf, stat, stat, stat, stat, stat,
                        pltpu.VMEM((2 * tq, V_HEAD_DIM), F32)],
        compiler_params=_params("parallel", "parallel", "arbitrary"),
        name="diff_attention",
    )(lq1, lk1, lq2, lk2, subln_g, q, k, v)


def _conv_kernel(prev_ref, cur_ref, next_ref, w_ref, b_ref, lg_ref, lb_ref, o_ref, ext_ref,
                 *, row_chunk, norm_chunk):
    i = pl.program_id(1)
    tm = cur_ref.shape[0]
    pad = (CONV_WIDTH - 1) // 2
    ext_ref[0:HALO_ROWS] = jnp.where(i > 0, prev_ref[...], 0.0)
    ext_ref[HALO_ROWS:HALO_ROWS + tm] = cur_ref[...]
    ext_ref[HALO_ROWS + tm:] = jnp.where(i < pl.num_programs(1) - 1, next_ref[...], 0.0)
    bias, ln_g, ln_b = b_ref[...], lg_ref[...], lb_ref[...]

    def channel_sum(x):
        return jnp.sum(jnp.sum(x, axis=2, keepdims=True), axis=1, keepdims=True)

    def conv_body(r, carry):
        r0 = r * row_chunk
        acc = jnp.broadcast_to(bias, (row_chunk,) + bias.shape)
        for t in range(CONV_WIDTH):
            off = HALO_ROWS - pad + t
            acc = acc + w_ref[t] * ext_ref[pl.ds(r0 + off, row_chunk)]
        o_ref[pl.ds(r0, row_chunk)] = acc
        return carry

    lax.fori_loop(0, tm // row_chunk, conv_body, 0)

    def norm_body(r, carry):
        rows = pl.ds(r * norm_chunk, norm_chunk)
        y = o_ref[rows]
        xc = y - channel_sum(y) * (1.0 / CONV_CH)
        var = channel_sum(xc * xc) * (1.0 / CONV_CH)
        y = xc * lax.rsqrt(var + EPS) * ln_g + ln_b
        o_ref[rows] = y * jax.nn.sigmoid(y)
        return carry

    lax.fori_loop(0, tm // norm_chunk, norm_body, 0)


def _halo_specs(tm, seq, width, col_map):
    per_tile = tm // HALO_ROWS
    last = seq // HALO_ROWS - 1
    prev = pl.BlockSpec((None, HALO_ROWS, width),
                        lambda b, i, *f: (b, jnp.maximum(i * per_tile - 1, 0), col_map(*f)))
    nxt = pl.BlockSpec((None, HALO_ROWS, width),
                       lambda b, i, *f: (b, jnp.minimum((i + 1) * per_tile, last), col_map(*f)))
    return prev, nxt


def _conv_module(glu, conv_dw_w, conv_dw_b, conv_ln_g, conv_ln_b):
    batch, seq, _ = glu.shape
    tm = _row_tile(seq)
    tile = (CONV_CH // 128, 128)
    per_tile = tm // HALO_ROWS
    last = seq // HALO_ROWS - 1
    vec = pl.BlockSpec(tile, lambda b, i: (0, 0))
    glu4 = glu.reshape((batch, seq) + tile)
    out = pl.pallas_call(
        functools.partial(_conv_kernel, row_chunk=16, norm_chunk=64),
        grid=(batch, seq // tm),
        in_specs=[
            pl.BlockSpec((None, HALO_ROWS) + tile,
                         lambda b, i: (b, jnp.maximum(i * per_tile - 1, 0), 0, 0)),
            pl.BlockSpec((None, tm) + tile, lambda b, i: (b, i, 0, 0)),
            pl.BlockSpec((None, HALO_ROWS) + tile,
                         lambda b, i: (b, jnp.minimum((i + 1) * per_tile, last), 0, 0)),
            pl.BlockSpec((CONV_WIDTH,) + tile, lambda b, i: (0, 0, 0)),
            vec, vec, vec,
        ],
        out_specs=pl.BlockSpec((None, tm) + tile, lambda b, i: (b, i, 0, 0)),
        out_shape=jax.ShapeDtypeStruct((batch, seq) + tile, F32),
        scratch_shapes=[pltpu.VMEM((tm + 2 * HALO_ROWS,) + tile, F32)],
        compiler_params=_params("parallel", "parallel"),
        name="conv_module",
    )(glu4, glu4, glu4, conv_dw_w.reshape((CONV_WIDTH,) + tile), conv_dw_b.reshape(tile),
      conv_ln_g.reshape(tile), conv_ln_b.reshape(tile))
    return out.reshape(batch, seq, CONV_CH)


def _merge_kernel(o_ref, c_ref, g_ref, wa_ref, wc_ref, m_ref):
    a = jnp.dot(o_ref[...], wa_ref[...], preferred_element_type=F32)
    cb = jnp.dot(c_ref[...].astype(BF16), wc_ref[...], preferred_element_type=F32)
    m = g_ref[:, :D_MODEL].astype(F32) * a + g_ref[:, D_MODEL:].astype(F32) * cb
    m_ref[...] = m.astype(BF16)


def _merge(o2, c2, gates, w_attn_proj, w_conv_proj, tm):
    rows = o2.shape[0]
    return pl.pallas_call(
        _merge_kernel,
        grid=(rows // tm,),
        in_specs=[
            pl.BlockSpec((tm, V_WIDTH), lambda i: (i, 0)),
            pl.BlockSpec((tm, CONV_CH), lambda i: (i, 0)),
            pl.BlockSpec((tm, N_BRANCH * D_MODEL), lambda i: (i, 0)),
            _resident((V_WIDTH, D_MODEL)),
            _resident((CONV_CH, D_MODEL)),
        ],
        out_specs=pl.BlockSpec((tm, D_MODEL), lambda i: (i, 0)),
        out_shape=jax.ShapeDtypeStruct((rows, D_MODEL), BF16),
        compiler_params=_params("parallel"),
        name="merge",
    )(o2, c2, gates, w_attn_proj, w_conv_proj)


def _out_proj_kernel(x_ref, m_ref, w_ref, y_ref):
    y_ref[...] = x_ref[...] + jnp.dot(m_ref[...], w_ref[...], preferred_element_type=F32)


def _out_proj(x2, m2, w_out, tm):
    rows = x2.shape[0]
    return pl.pallas_call(
        _out_proj_kernel,
        grid=(rows // tm,),
        in_specs=[
            pl.BlockSpec((tm, D_MODEL), lambda i: (i, 0)),
            pl.BlockSpec((tm, D_MODEL), lambda i: (i, 0)),
            _resident((D_MODEL, D_MODEL)),
        ],
        out_specs=pl.BlockSpec((tm, D_MODEL), lambda i: (i, 0)),
        out_shape=jax.ShapeDtypeStruct((rows, D_MODEL), F32),
        compiler_params=_params("parallel"),
        name="out_proj",
    )(x2, m2, w_out)


def _ffn_up_kernel(x_ref, g_ref, w_ref, up_ref, h_ref):
    @pl.when(pl.program_id(1) == 0)
    def _():
        x = x_ref[...]
        ms = jnp.mean(x * x, axis=-1, keepdims=True)
        h_ref[...] = (x * lax.rsqrt(ms + EPS) * g_ref[...]).astype(BF16)

    up_ref[...] = jnp.dot(h_ref[...], w_ref[...], preferred_element_type=F32).astype(BF16)


def _ffn_up(x2, norm2_g, w_up, tm):
    rows = x2.shape[0]
    tn = 1408
    return pl.pallas_call(
        _ffn_up_kernel,
        grid=(rows // tm, 2 * FFN_DIM // tn),
        in_specs=[
            pl.BlockSpec((tm, D_MODEL), lambda i, j: (i, 0)),
            pl.BlockSpec((1, D_MODEL), lambda i, j: (0, 0)),
            pl.BlockSpec((D_MODEL, tn), lambda i, j: (0, j)),
        ],
        out_specs=pl.BlockSpec((tm, tn), lambda i, j: (i, j)),
        out_shape=jax.ShapeDtypeStruct((rows, 2 * FFN_DIM), BF16),
        scratch_shapes=[pltpu.VMEM((tm, D_MODEL), BF16)],
        compiler_params=_params("parallel", "arbitrary"),
        name="ffn_up",
    )(x2, norm2_g, w_up)


def _conv3(prev_ref, cur_ref, next_ref, w_ref, b_ref, first, last):
    cur = cur_ref[...].astype(F32)
    tm = cur.shape[0]
    row = lax.broadcasted_iota(jnp.int32, cur.shape, 0)
    before = jnp.where(first, 0.0, prev_ref[HALO_ROWS - 1:HALO_ROWS, :].astype(F32))
    after = jnp.where(last, 0.0, next_ref[0:1, :].astype(F32))
    up_m1 = jnp.where(row == 0, before, pltpu.roll(cur, 1, 0))
    up_p1 = jnp.where(row == tm - 1, after, pltpu.roll(cur, tm - 1, 0))
    return w_ref[0:1, :] * up_m1 + w_ref[1:2, :] * cur + w_ref[2:3, :] * up_p1 + b_ref[...]


def _ffn_down_kernel(pa_ref, a_ref, na_ref, pb_ref, b_ref, nb_ref, wa_ref, ba_ref, wb_ref, bb_ref,
                     wd_ref, x_ref, y_ref):
    i = pl.program_id(1)
    f = pl.program_id(2)
    first = i == 0
    last = i == pl.num_programs(1) - 1
    fa = _conv3(pa_ref, a_ref, na_ref, wa_ref, ba_ref, first, last)
    fb = _conv3(pb_ref, b_ref, nb_ref, wb_ref, bb_ref, first, last)
    act = (fa * jax.nn.sigmoid(fa) * fb).astype(BF16)
    contrib = jnp.dot(act, wd_ref[...], preferred_element_type=F32)

    @pl.when(f == 0)
    def _():
        y_ref[...] = x_ref[...] + contrib

    @pl.when(f > 0)
    def _():
        y_ref[...] += contrib


def _ffn_down(up, x1, ffn_dw_w, ffn_dw_b, w_down):
    batch, seq, _ = up.shape
    tm = _row_tile(seq)
    tf = 512
    nf = FFN_DIM // tf
    prev_a, next_a = _halo_specs(tm, seq, tf, lambda f: f)
    prev_b, next_b = _halo_specs(tm, seq, tf, lambda f: f + nf)
    return pl.pallas_call(
        _ffn_down_kernel,
        grid=(batch, seq // tm, nf),
        in_specs=[
            prev_a, pl.BlockSpec((None, tm, tf), lambda b, i, f: (b, i, f)), next_a,
            prev_b, pl.BlockSpec((None, tm, tf), lambda b, i, f: (b, i, f + nf)), next_b,
            pl.BlockSpec((FFN_CONV_WIDTH, tf), lambda b, i, f: (0, f)),
            pl.BlockSpec((1, tf), lambda b, i, f: (0, f)),
            pl.BlockSpec((FFN_CONV_WIDTH, tf), lambda b, i, f: (0, f + nf)),
            pl.BlockSpec((1, tf), lambda b, i, f: (0, f + nf)),
            pl.BlockSpec((tf, D_MODEL), lambda b, i, f: (f, 0)),
            pl.BlockSpec((None, tm, D_MODEL), lambda b, i, f: (b, i, 0)),
        ],
        out_specs=pl.BlockSpec((None, tm, D_MODEL), lambda b, i, f: (b, i, 0)),
        out_shape=jax.ShapeDtypeStruct((batch, seq, D_MODEL), F32),
        compiler_params=_params("parallel", "parallel", "arbitrary"),
        name="ffn_down",
    )(up, up, up, up, up, up, ffn_dw_w, ffn_dw_b, ffn_dw_w, ffn_dw_b, w_down, x1)


def _rope_tables(seq):
    inv_freq = 1.0 / (ROPE_THETA ** (jnp.arange(0, HEAD_DIM, 2, dtype=F32) / HEAD_DIM))
    ang = jnp.arange(seq, dtype=F32)[:, None] * inv_freq[None, :]
    ang = jnp.concatenate([ang, ang], axis=-1)
    sign = jnp.where(jnp.arange(HEAD_DIM) < HEAD_DIM // 2, -1.0, 1.0).astype(F32)
    return jnp.cos(ang), jnp.sin(ang) * sign


def _encoder_layer(x, layer_idx, p):
    batch, seq, _ = x.shape
    rows = batch * seq
    tm = _row_tile(seq)
    lam_init = 0.8 - 0.6 * math.exp(-0.3 * layer_idx)
    cos, sin_signed = _rope_tables(seq)
    x2 = x.reshape(rows, D_MODEL)

    q, k, v, glu, gates = _in_proj(x2, seq, p["norm1_g"], p["w_in"], p["b_gate"], p["q_norm_g"],
                                   p["k_norm_g"], cos, sin_signed)
    o = _attention(q.reshape(batch, seq, QK_WIDTH), k.reshape(batch, seq, QK_WIDTH),
                   v.reshape(batch, seq, V_WIDTH), p["lambda_q1"], p["lambda_k1"], p["lambda_q2"],
                   p["lambda_k2"], p["subln_g"], lam_init)
    c = _conv_module(glu.reshape(batch, seq, CONV_CH), p["conv_dw_w"], p["conv_dw_b"],
                     p["conv_ln_g"], p["conv_ln_b"])
    m = _merge(o.reshape(rows, V_WIDTH), c.reshape(rows, CONV_CH), gates, p["w_attn_proj"],
               p["w_conv_proj"], tm)
    x1 = _out_proj(x2, m, p["w_out"], tm)
    up = _ffn_up(x1, p["norm2_g"], p["w_up"], tm)
    y = _ffn_down(up.reshape(batch, seq, 2 * FFN_DIM), x1.reshape(batch, seq, D_MODEL),
                  p["ffn_dw_w"], p["ffn_dw_b"], p["w_down"])
    return y


_MATRICES = ("w_in", "w_attn_proj", "w_conv_proj", "w_out", "w_up", "w_down")
_ROW_VECTORS = ("norm1_g", "b_gate", "q_norm_g", "k_norm_g", "lambda_q1", "lambda_k1", "lambda_q2",
                "lambda_k2", "subln_g", "conv_dw_b", "conv_ln_g", "conv_ln_b", "norm2_g", "ffn_dw_b")


def kernel(x_prompt, x_sample, norm1_g, w_in, b_gate, q_norm_g, k_norm_g, lambda_q1, lambda_k1, lambda_q2, lambda_k2, subln_g, w_attn_proj, conv_dw_w, conv_dw_b, conv_ln_g, conv_ln_b, w_conv_proj, w_out, norm2_g, w_up, ffn_dw_w, ffn_dw_b, w_down):
    stacked = dict(norm1_g=norm1_g, w_in=w_in, b_gate=b_gate, q_norm_g=q_norm_g, k_norm_g=k_norm_g,
                   lambda_q1=lambda_q1, lambda_k1=lambda_k1, lambda_q2=lambda_q2,
                   lambda_k2=lambda_k2, subln_g=subln_g, w_attn_proj=w_attn_proj,
                   conv_dw_w=conv_dw_w, conv_dw_b=conv_dw_b, conv_ln_g=conv_ln_g,
                   conv_ln_b=conv_ln_b, w_conv_proj=w_conv_proj, w_out=w_out, norm2_g=norm2_g,
                   w_up=w_up, ffn_dw_w=ffn_dw_w, ffn_dw_b=ffn_dw_b, w_down=w_down)
    depth = w_in.shape[0]
    layers = []
    for l in range(depth):
        p = {}
        for name, value in stacked.items():
            value = value[l]
            if name in _MATRICES:
                value = value.astype(BF16)
            elif name in _ROW_VECTORS:
                value = value.astype(F32)[None, :]
            else:
                value = value.astype(F32)
            p[name] = value
        layers.append(p)

    def run(x):
        for l, p in enumerate(layers):
            x = _encoder_layer(x, l, p)
        return x

    return run(x_prompt), run(x_sample)
```

```python
import functools
import math

import jax
import jax.numpy as jnp
from jax import lax
from jax.experimental import pallas as pl
from jax.experimental.pallas import tpu as pltpu

F32 = jnp.float32
BF16 = jnp.bfloat16

D_MODEL = 2048
N_HEADS = 8
HEAD_DIM = 128
V_HEAD_DIM = 2 * HEAD_DIM
QK_WIDTH = 2 * N_HEADS * HEAD_DIM
V_WIDTH = N_HEADS * V_HEAD_DIM
CONV_CH = 1024
CONV_WIDTH = 31
N_BRANCH = 2
FFN_DIM = 5632
FFN_CONV_WIDTH = 3
ROPE_THETA = 10000.0
EPS = 1e-6
IN_WIDTH = 2 * QK_WIDTH + V_WIDTH + 2 * CONV_CH + N_BRANCH * D_MODEL

VMEM_LIMIT_BYTES = 56 * 1024 * 1024
HALO_ROWS = 16
IN_COL_TILE = 1024
LOG2E = math.log2(math.e)


def _params(*semantics):
    return pltpu.CompilerParams(dimension_semantics=semantics, vmem_limit_bytes=VMEM_LIMIT_BYTES)


def _resident(shape):
    return pl.BlockSpec(shape, lambda *_: (0,) * len(shape), pipeline_mode=pl.Buffered(1))


def _row_tile(seq):
    return min(512, seq)


def _qk_epilogue(z, gain, cos, sin_signed, out_scale):
    outs = []
    for h in range(z.shape[1] // HEAD_DIM):
        zh = z[:, h * HEAD_DIM:(h + 1) * HEAD_DIM]
        ms = jnp.mean(zh * zh, axis=-1, keepdims=True)
        y = zh * lax.rsqrt(ms + EPS) * gain
        y = y * cos + pltpu.roll(y, HEAD_DIM // 2, 1) * sin_signed
        outs.append((y * out_scale).astype(BF16))
    return jnp.concatenate(outs, axis=1)


def _in_proj_kernel(x_ref, g1_ref, w_ref, bg_ref, qg_ref, kg_ref, cos_ref, sin_ref,
                    q_ref, k_ref, v_ref, glu_ref, gate_ref, h_ref, ua_ref, *, q_scale):
    j = pl.program_id(1)

    @pl.when(j == 0)
    def _():
        x = x_ref[...]
        ms = jnp.mean(x * x, axis=-1, keepdims=True)
        h_ref[...] = (x * lax.rsqrt(ms + EPS) * g1_ref[...]).astype(BF16)

    z = jnp.dot(h_ref[...], w_ref[...], preferred_element_type=F32)

    @pl.when(j < 2)
    def _():
        q_ref[...] = _qk_epilogue(z, qg_ref[...], cos_ref[...], sin_ref[...], q_scale)

    @pl.when((j >= 2) & (j < 4))
    def _():
        k_ref[...] = _qk_epilogue(z, kg_ref[...], cos_ref[...], sin_ref[...], 1.0)

    @pl.when((j >= 4) & (j < 6))
    def _():
        v_ref[...] = z.astype(BF16)

    @pl.when(j == 6)
    def _():
        ua_ref[...] = z

    @pl.when(j == 7)
    def _():
        glu_ref[...] = ua_ref[...] * jax.nn.sigmoid(z)

    @pl.when(j >= 8)
    def _():
        gate_ref[...] = jax.nn.sigmoid(z + bg_ref[...]).astype(BF16)


def _in_proj(x2, seq, norm1_g, w_in, b_gate, q_norm_g, k_norm_g, cos, sin_signed):
    rows = x2.shape[0]
    tm = _row_tile(seq)
    tn = IN_COL_TILE
    seq_tiles = seq // tm

    def clamp(j, lo, n):
        return jnp.clip(j - lo, 0, n - 1)

    kernel = functools.partial(_in_proj_kernel, q_scale=HEAD_DIM ** -0.5 * LOG2E)
    out_shapes = (
        jax.ShapeDtypeStruct((rows, QK_WIDTH), BF16),
        jax.ShapeDtypeStruct((rows, QK_WIDTH), BF16),
        jax.ShapeDtypeStruct((rows, V_WIDTH), BF16),
        jax.ShapeDtypeStruct((rows, CONV_CH), F32),
        jax.ShapeDtypeStruct((rows, N_BRANCH * D_MODEL), BF16),
    )
    return pl.pallas_call(
        kernel,
        grid=(rows // tm, IN_WIDTH // tn),
        in_specs=[
            pl.BlockSpec((tm, D_MODEL), lambda i, j: (i, 0)),
            pl.BlockSpec((1, D_MODEL), lambda i, j: (0, 0)),
            pl.BlockSpec((D_MODEL, tn), lambda i, j: (0, j)),
            pl.BlockSpec((1, tn), lambda i, j: (0, clamp(j, 8, 4))),
            pl.BlockSpec((1, HEAD_DIM), lambda i, j: (0, 0)),
            pl.BlockSpec((1, HEAD_DIM), lambda i, j: (0, 0)),
            pl.BlockSpec((tm, HEAD_DIM), lambda i, j: (i % seq_tiles, 0)),
            pl.BlockSpec((tm, HEAD_DIM), lambda i, j: (i % seq_tiles, 0)),
        ],
        out_specs=(
            pl.BlockSpec((tm, tn), lambda i, j: (i, clamp(j, 0, 2))),
            pl.BlockSpec((tm, tn), lambda i, j: (i, clamp(j, 2, 2))),
            pl.BlockSpec((tm, tn), lambda i, j: (i, clamp(j, 4, 2))),
            pl.BlockSpec((tm, CONV_CH), lambda i, j: (i, 0)),
            pl.BlockSpec((tm, tn), lambda i, j: (i, clamp(j, 8, 4))),
        ),
        out_shape=out_shapes,
        scratch_shapes=[pltpu.VMEM((tm, D_MODEL), BF16), pltpu.VMEM((tm, CONV_CH), F32)],
        compiler_params=_params("parallel", "arbitrary"),
        name="in_proj",
    )(x2, norm1_g, w_in, b_gate, q_norm_g, k_norm_g, cos, sin_signed)


VT_ROWS = V_HEAD_DIM + 16


def _attn_kernel(lq1_ref, lk1_ref, lq2_ref, lk2_ref, sg_ref, qt_ref, k_ref, vt_ref, o_ref,
                 s0_ref, s1_ref, x0_ref, x1_ref, p0_ref, p1_ref, a0_ref, a1_ref, m_ref, acc_ref,
                 *, tq, k_chunk, lam_init):
    n_chunks = k_ref.shape[0] // k_chunk
    n_units = qt_ref.shape[0] * n_chunks
    s_bufs, p_bufs = (s0_ref, s1_ref), (p0_ref, p1_ref)
    a_bufs, max_bufs = (a0_ref, a1_ref), (x0_ref, x1_ref)

    def scores(u, slot):
        c0 = pl.multiple_of((u % n_chunks) * k_chunk, k_chunk)
        for sub in range(2):
            dims = slice(sub * HEAD_DIM, (sub + 1) * HEAD_DIM)
            st = jnp.dot(k_ref[pl.ds(c0, k_chunk), dims], qt_ref[u // n_chunks, dims, :],
                         preferred_element_type=F32)
            s_bufs[slot][sub] = st
            max_bufs[slot][sub] = jnp.broadcast_to(jnp.max(st, axis=0, keepdims=True), (8, tq))

    def softmax(u, slot):
        m_old = m_ref[...]
        if slot == 0:
            m_old = m_old + jnp.where(u % n_chunks == 0, -jnp.inf, 0.0)
        m_new = jnp.maximum(m_old, max_bufs[slot][...])
        a_bufs[slot][...] = jnp.exp2(m_old - m_new)
        m_ref[...] = m_new
        for sub in range(2):
            d = s_bufs[slot][sub] - m_new[sub, 0:1, :]
            p_bufs[slot][sub] = jnp.exp2(d).astype(BF16)

    def values(u, slot):
        vt = vt_ref[u % n_chunks]
        for sub in range(2):
            pv = jnp.dot(vt, p_bufs[slot][sub], preferred_element_type=F32)
            acc_ref[sub] = a_bufs[slot][sub, 0:1, :] * acc_ref[sub] + pv

    def normalise(tile):
        lam = (jnp.exp(jnp.sum(lq1_ref[...] * lk1_ref[...], axis=-1, keepdims=True))
               - jnp.exp(jnp.sum(lq2_ref[...] * lk2_ref[...], axis=-1, keepdims=True)) + lam_init)
        o0 = acc_ref[0, :V_HEAD_DIM, :] / acc_ref[0, V_HEAD_DIM:V_HEAD_DIM + 1, :]
        o1 = acc_ref[1, :V_HEAD_DIM, :] / acc_ref[1, V_HEAD_DIM:V_HEAD_DIM + 1, :]
        o = o0 - lam * o1
        ms = jnp.mean(o * o, axis=0, keepdims=True)
        gain = jnp.concatenate([sg_ref[...]] * (tq // HEAD_DIM), axis=1)
        o = o * lax.rsqrt(ms + EPS) * gain * (1.0 - lam_init)
        o_ref[tile] = o.astype(BF16)

    m_ref[...] = jnp.full(m_ref.shape, -jnp.inf, F32)
    acc_ref[...] = jnp.zeros(acc_ref.shape, F32)

    scores(0, 0)
    scores(1, 1)
    softmax(0, 0)

    def body(t, carry):
        u = 2 * t + 2
        scores(u, 0)
        softmax(u - 1, 1)
        values(u - 2, 0)
        scores(u + 1, 1)
        softmax(u, 0)
        values(u - 1, 1)

        @pl.when(u % n_chunks == 0)
        def _():
            normalise(u // n_chunks - 1)

        return carry

    lax.fori_loop(0, n_units // 2 - 1, body, 0)
    softmax(n_units - 1, 1)
    values(n_units - 2, 0)
    values(n_units - 1, 1)
    normalise(n_units // n_chunks - 1)


def _attention(q, k, v, lq1, lk1, lq2, lk2, subln_g, lam_init):
    batch, seq, _ = q.shape
    tq = min(512, seq)
    tq_group = min(2048, seq)
    k_chunk = min(512, seq // 2)
    assert (seq // k_chunk) % 2 == 0
    qt = q.reshape(batch, seq // tq, tq, N_HEADS, 2 * HEAD_DIM).transpose(0, 3, 1, 4, 2)
    vt = v.reshape(batch, seq // k_chunk, k_chunk, N_HEADS, V_HEAD_DIM).transpose(0, 3, 1, 4, 2)
    ones = jnp.ones((batch, N_HEADS, seq // k_chunk, VT_ROWS - V_HEAD_DIM, k_chunk), BF16)
    vt = jnp.concatenate([vt, ones], axis=3)
    gain = jnp.broadcast_to(subln_g.reshape(V_HEAD_DIM, 1), (V_HEAD_DIM, HEAD_DIM))
    vec = pl.BlockSpec((1, HEAD_DIM), lambda b, h, i: (0, 0))
    kernel = functools.partial(_attn_kernel, tq=tq, k_chunk=k_chunk, lam_init=lam_init)
    s_buf = pltpu.VMEM((2, k_chunk, tq), F32)
    p_buf = pltpu.VMEM((2, k_chunk, tq), BF16)
    stat = pltpu.VMEM((2, 8, tq), F32)
    ot = pl.pallas_call(
        kernel,
        grid=(batch, N_HEADS, seq // tq_group),
        in_specs=[
            vec, vec, vec, vec,
            pl.BlockSpec((V_HEAD_DIM, HEAD_DIM), lambda b, h, i: (0, 0)),
            pl.BlockSpec((None, None, tq_group // tq, 2 * HEAD_DIM, tq),
                         lambda b, h, i: (b, h, i, 0, 0)),
            pl.BlockSpec((None, seq, 2 * HEAD_DIM), lambda b, h, i: (b, 0, h),
                         pipeline_mode=pl.Buffered(1)),
            pl.BlockSpec((None, None, seq // k_chunk, VT_ROWS, k_chunk),
                         lambda b, h, i: (b, h, 0, 0, 0), pipeline_mode=pl.Buffered(1)),
        ],
        out_specs=pl.BlockSpec((None, None, tq_group // tq, V_HEAD_DIM, tq),
                               lambda b, h, i: (b, h, i, 0, 0)),
        out_shape=jax.ShapeDtypeStruct((batch, N_HEADS, seq // tq, V_HEAD_DIM, tq), BF16),
        scratch_shapes=[s_buf, s_buf, stat, stat, p_buf, p_buf, stat, stat, stat,
                        pltpu.VMEM((2, VT_ROWS, tq), F32)],
        compiler_params=_params("parallel", "parallel", "arbitrary"),
        name="diff_attention",
    )(lq1, lk1, lq2, lk2, gain, qt, k, vt)
    return ot.transpose(0, 2, 4, 1, 3).reshape(batch, seq, V_WIDTH)


def _conv_kernel(prev_ref, cur_ref, next_ref, w_ref, b_ref, lg_ref, lb_ref, o_ref, ext_ref,
                 *, row_chunk, norm_chunk):
    i = pl.program_id(1)
    tm = cur_ref.shape[0]
    pad = (CONV_WIDTH - 1) // 2
    ext_ref[0:HALO_ROWS] = jnp.where(i > 0, prev_ref[...], 0.0)
    ext_ref[HALO_ROWS:HALO_ROWS + tm] = cur_ref[...]
    ext_ref[HALO_ROWS + tm:] = jnp.where(i < pl.num_programs(1) - 1, next_ref[...], 0.0)
    bias, ln_g, ln_b = b_ref[...], lg_ref[...], lb_ref[...]

    def channel_sum(x):
        return jnp.sum(jnp.sum(x, axis=2, keepdims=True), axis=1, keepdims=True)

    def conv_body(r, carry):
        r0 = r * row_chunk
        acc = jnp.broadcast_to(bias, (row_chunk,) + bias.shape)
        for t in range(CONV_WIDTH):
            off = HALO_ROWS - pad + t
            acc = acc + w_ref[t] * ext_ref[pl.ds(r0 + off, row_chunk)]
        o_ref[pl.ds(r0, row_chunk)] = acc
        return carry

    lax.fori_loop(0, tm // row_chunk, conv_body, 0)

    def norm_body(r, carry):
        rows = pl.ds(r * norm_chunk, norm_chunk)
        y = o_ref[rows]
        xc = y - channel_sum(y) * (1.0 / CONV_CH)
        var = channel_sum(xc * xc) * (1.0 / CONV_CH)
        y = xc * lax.rsqrt(var + EPS) * ln_g + ln_b
        o_ref[rows] = y * jax.nn.sigmoid(y)
        return carry

    lax.fori_loop(0, tm // norm_chunk, norm_body, 0)


def _halo_specs(tm, seq, width, col_map):
    per_tile = tm // HALO_ROWS
    last = seq // HALO_ROWS - 1
    prev = pl.BlockSpec((None, HALO_ROWS, width),
                        lambda b, i, *f: (b, jnp.maximum(i * per_tile - 1, 0), col_map(*f)))
    nxt = pl.BlockSpec((None, HALO_ROWS, width),
                       lambda b, i, *f: (b, jnp.minimum((i + 1) * per_tile, last), col_map(*f)))
    return prev, nxt


def _conv_module(glu, conv_dw_w, conv_dw_b, conv_ln_g, conv_ln_b):
    batch, seq, _ = glu.shape
    tm = _row_tile(seq)
    tile = (CONV_CH // 128, 128)
    per_tile = tm // HALO_ROWS
    last = seq // HALO_ROWS - 1
    vec = pl.BlockSpec(tile, lambda b, i: (0, 0))
    glu4 = glu.reshape((batch, seq) + tile)
    out = pl.pallas_call(
        functools.partial(_conv_kernel, row_chunk=16, norm_chunk=64),
        grid=(batch, seq // tm),
        in_specs=[
            pl.BlockSpec((None, HALO_ROWS) + tile,
                         lambda b, i: (b, jnp.maximum(i * per_tile - 1, 0), 0, 0)),
            pl.BlockSpec((None, tm) + tile, lambda b, i: (b, i, 0, 0)),
            pl.BlockSpec((None, HALO_ROWS) + tile,
                         lambda b, i: (b, jnp.minimum((i + 1) * per_tile, last), 0, 0)),
            pl.BlockSpec((CONV_WIDTH,) + tile, lambda b, i: (0, 0, 0)),
            vec, vec, vec,
        ],
        out_specs=pl.BlockSpec((None, tm) + tile, lambda b, i: (b, i, 0, 0)),
        out_shape=jax.ShapeDtypeStruct((batch, seq) + tile, F32),
        scratch_shapes=[pltpu.VMEM((tm + 2 * HALO_ROWS,) + tile, F32)],
        compiler_params=_params("parallel", "parallel"),
        name="conv_module",
    )(glu4, glu4, glu4, conv_dw_w.reshape((CONV_WIDTH,) + tile), conv_dw_b.reshape(tile),
      conv_ln_g.reshape(tile), conv_ln_b.reshape(tile))
    return out.reshape(batch, seq, CONV_CH)


def _merge_kernel(o_ref, c_ref, g_ref, wa_ref, wc_ref, m_ref):
    a = jnp.dot(o_ref[...], wa_ref[...], preferred_element_type=F32)
    cb = jnp.dot(c_ref[...].astype(BF16), wc_ref[...], preferred_element_type=F32)
    m = g_ref[:, :D_MODEL].astype(F32) * a + g_ref[:, D_MODEL:].astype(F32) * cb
    m_ref[...] = m.astype(BF16)


def _merge(o2, c2, gates, w_attn_proj, w_conv_proj, tm):
    rows = o2.shape[0]
    return pl.pallas_call(
        _merge_kernel,
        grid=(rows // tm,),
        in_specs=[
            pl.BlockSpec((tm, V_WIDTH), lambda i: (i, 0)),
            pl.BlockSpec((tm, CONV_CH), lambda i: (i, 0)),
            pl.BlockSpec((tm, N_BRANCH * D_MODEL), lambda i: (i, 0)),
            _resident((V_WIDTH, D_MODEL)),
            _resident((CONV_CH, D_MODEL)),
        ],
        out_specs=pl.BlockSpec((tm, D_MODEL), lambda i: (i, 0)),
        out_shape=jax.ShapeDtypeStruct((rows, D_MODEL), BF16),
        compiler_params=_params("parallel"),
        name="merge",
    )(o2, c2, gates, w_attn_proj, w_conv_proj)


def _out_proj_kernel(x_ref, m_ref, w_ref, y_ref):
    y_ref[...] = x_ref[...] + jnp.dot(m_ref[...], w_ref[...], preferred_element_type=F32)


def _out_proj(x2, m2, w_out, tm):
    rows = x2.shape[0]
    return pl.pallas_call(
        _out_proj_kernel,
        grid=(rows // tm,),
        in_specs=[
            pl.BlockSpec((tm, D_MODEL), lambda i: (i, 0)),
            pl.BlockSpec((tm, D_MODEL), lambda i: (i, 0)),
            _resident((D_MODEL, D_MODEL)),
        ],
        out_specs=pl.BlockSpec((tm, D_MODEL), lambda i: (i, 0)),
        out_shape=jax.ShapeDtypeStruct((rows, D_MODEL), F32),
        compiler_params=_params("parallel"),
        name="out_proj",
    )(x2, m2, w_out)


def _ffn_up_kernel(x_ref, g_ref, w_ref, up_ref, h_ref):
    @pl.when(pl.program_id(1) == 0)
    def _():
        x = x_ref[...]
        ms = jnp.mean(x * x, axis=-1, keepdims=True)
        h_ref[...] = (x * lax.rsqrt(ms + EPS) * g_ref[...]).astype(BF16)

    up_ref[...] = jnp.dot(h_ref[...], w_ref[...], preferred_element_type=F32).astype(BF16)


def _ffn_up(x2, norm2_g, w_up, tm):
    rows = x2.shape[0]
    tn = 1408
    return pl.pallas_call(
        _ffn_up_kernel,
        grid=(rows // tm, 2 * FFN_DIM // tn),
        in_specs=[
            pl.BlockSpec((tm, D_MODEL), lambda i, j: (i, 0)),
            pl.BlockSpec((1, D_MODEL), lambda i, j: (0, 0)),
            pl.BlockSpec((D_MODEL, tn), lambda i, j: (0, j)),
        ],
        out_specs=pl.BlockSpec((tm, tn), lambda i, j: (i, j)),
        out_shape=jax.ShapeDtypeStruct((rows, 2 * FFN_DIM), BF16),
        scratch_shapes=[pltpu.VMEM((tm, D_MODEL), BF16)],
        compiler_params=_params("parallel", "arbitrary"),
        name="ffn_up",
    )(x2, norm2_g, w_up)


def _conv3(prev_ref, cur_ref, next_ref, w_ref, b_ref, first, last):
    cur = cur_ref[...].astype(F32)
    tm = cur.shape[0]
    row = lax.broadcasted_iota(jnp.int32, cur.shape, 0)
    before = jnp.where(first, 0.0, prev_ref[HALO_ROWS - 1:HALO_ROWS, :].astype(F32))
    after = jnp.where(last, 0.0, next_ref[0:1, :].astype(F32))
    up_m1 = jnp.where(row == 0, before, pltpu.roll(cur, 1, 0))
    up_p1 = jnp.where(row == tm - 1, after, pltpu.roll(cur, tm - 1, 0))
    return w_ref[0:1, :] * up_m1 + w_ref[1:2, :] * cur + w_ref[2:3, :] * up_p1 + b_ref[...]


def _ffn_down_kernel(pa_ref, a_ref, na_ref, pb_ref, b_ref, nb_ref, wa_ref, ba_ref, wb_ref, bb_ref,
                     wd_ref, x_ref, y_ref):
    i = pl.program_id(1)
    f = pl.program_id(2)
    first = i == 0
    last = i == pl.num_programs(1) - 1
    fa = _conv3(pa_ref, a_ref, na_ref, wa_ref, ba_ref, first, last)
    fb = _conv3(pb_ref, b_ref, nb_ref, wb_ref, bb_ref, first, last)
    act = (fa * jax.nn.sigmoid(fa) * fb).astype(BF16)
    contrib = jnp.dot(act, wd_ref[...], preferred_element_type=F32)

    @pl.when(f == 0)
    def _():
        y_ref[...] = x_ref[...] + contrib

    @pl.when(f > 0)
    def _():
        y_ref[...] += contrib


def _ffn_down(up, x1, ffn_dw_w, ffn_dw_b, w_down):
    batch, seq, _ = up.shape
    tm = _row_tile(seq)
    tf = 512
    nf = FFN_DIM // tf
    prev_a, next_a = _halo_specs(tm, seq, tf, lambda f: f)
    prev_b, next_b = _halo_specs(tm, seq, tf, lambda f: f + nf)
    return pl.pallas_call(
        _ffn_down_kernel,
        grid=(batch, seq // tm, nf),
        in_specs=[
            prev_a, pl.BlockSpec((None, tm, tf), lambda b, i, f: (b, i, f)), next_a,
            prev_b, pl.BlockSpec((None, tm, tf), lambda b, i, f: (b, i, f + nf)), next_b,
            pl.BlockSpec((FFN_CONV_WIDTH, tf), lambda b, i, f: (0, f)),
            pl.BlockSpec((1, tf), lambda b, i, f: (0, f)),
            pl.BlockSpec((FFN_CONV_WIDTH, tf), lambda b, i, f: (0, f + nf)),
            pl.BlockSpec((1, tf), lambda b, i, f: (0, f + nf)),
            pl.BlockSpec((tf, D_MODEL), lambda b, i, f: (f, 0)),
            pl.BlockSpec((None, tm, D_MODEL), lambda b, i, f: (b, i, 0)),
        ],
        out_specs=pl.BlockSpec((None, tm, D_MODEL), lambda b, i, f: (b, i, 0)),
        out_shape=jax.ShapeDtypeStruct((batch, seq, D_MODEL), F32),
        compiler_params=_params("parallel", "parallel", "arbitrary"),
        name="ffn_down",
    )(up, up, up, up, up, up, ffn_dw_w, ffn_dw_b, ffn_dw_w, ffn_dw_b, w_down, x1)


def _rope_tables(seq):
    inv_freq = 1.0 / (ROPE_THETA ** (jnp.arange(0, HEAD_DIM, 2, dtype=F32) / HEAD_DIM))
    ang = jnp.arange(seq, dtype=F32)[:, None] * inv_freq[None, :]
    ang = jnp.concatenate([ang, ang], axis=-1)
    sign = jnp.where(jnp.arange(HEAD_DIM) < HEAD_DIM // 2, -1.0, 1.0).astype(F32)
    return jnp.cos(ang), jnp.sin(ang) * sign


def _encoder_layer(x, layer_idx, p):
    batch, seq, _ = x.shape
    rows = batch * seq
    tm = _row_tile(seq)
    lam_init = 0.8 - 0.6 * math.exp(-0.3 * layer_idx)
    cos, sin_signed = _rope_tables(seq)
    x2 = x.reshape(rows, D_MODEL)

    q, k, v, glu, gates = _in_proj(x2, seq, p["norm1_g"], p["w_in"], p["b_gate"], p["q_norm_g"],
                                   p["k_norm_g"], cos, sin_signed)
    o = _attention(q.reshape(batch, seq, QK_WIDTH), k.reshape(batch, seq, QK_WIDTH),
                   v.reshape(batch, seq, V_WIDTH), p["lambda_q1"], p["lambda_k1"], p["lambda_q2"],
                   p["lambda_k2"], p["subln_g"], lam_init)
    c = _conv_module(glu.reshape(batch, seq, CONV_CH), p["conv_dw_w"], p["conv_dw_b"],
                     p["conv_ln_g"], p["conv_ln_b"])
    m = _merge(o.reshape(rows, V_WIDTH), c.reshape(rows, CONV_CH), gates, p["w_attn_proj"],
               p["w_conv_proj"], tm)
    x1 = _out_proj(x2, m, p["w_out"], tm)
    up = _ffn_up(x1, p["norm2_g"], p["w_up"], tm)
    y = _ffn_down(up.reshape(batch, seq, 2 * FFN_DIM), x1.reshape(batch, seq, D_MODEL),
                  p["ffn_dw_w"], p["ffn_dw_b"], p["w_down"])
    return y


_MATRICES = ("w_in", "w_attn_proj", "w_conv_proj", "w_out", "w_up", "w_down")
_ROW_VECTORS = ("norm1_g", "b_gate", "q_norm_g", "k_norm_g", "lambda_q1", "lambda_k1", "lambda_q2",
                "lambda_k2", "subln_g", "conv_dw_b", "conv_ln_g", "conv_ln_b", "norm2_g", "ffn_dw_b")


def kernel(x_prompt, x_sample, norm1_g, w_in, b_gate, q_norm_g, k_norm_g, lambda_q1, lambda_k1, lambda_q2, lambda_k2, subln_g, w_attn_proj, conv_dw_w, conv_dw_b, conv_ln_g, conv_ln_b, w_conv_proj, w_out, norm2_g, w_up, ffn_dw_w, ffn_dw_b, w_down):
    stacked = dict(norm1_g=norm1_g, w_in=w_in, b_gate=b_gate, q_norm_g=q_norm_g, k_norm_g=k_norm_g,
                   lambda_q1=lambda_q1, lambda_k1=lambda_k1, lambda_q2=lambda_q2,
                   lambda_k2=lambda_k2, subln_g=subln_g, w_attn_proj=w_attn_proj,
                   conv_dw_w=conv_dw_w, conv_dw_b=conv_dw_b, conv_ln_g=conv_ln_g,
                   conv_ln_b=conv_ln_b, w_conv_proj=w_conv_proj, w_out=w_out, norm2_g=norm2_g,
                   w_up=w_up, ffn_dw_w=ffn_dw_w, ffn_dw_b=ffn_dw_b, w_down=w_down)
    depth = w_in.shape[0]
    layers = []
    for l in range(depth):
        p = {}
        for name, value in stacked.items():
            value = value[l]
            if name in _MATRICES:
                value = value.astype(BF16)
            elif name in _ROW_VECTORS:
                value = value.astype(F32)[None, :]
            else:
                value = value.astype(F32)
            p[name] = value
        layers.append(p)

    def run(x):
        for l, p in enumerate(layers):
            x = _encoder_layer(x, l, p)
        return x

    return run(x_prompt), run(x_sample)
```

```python
import functools
import math

import jax
import jax.numpy as jnp
from jax import lax
from jax.experimental import pallas as pl
from jax.experimental.pallas import tpu as pltpu

F32 = jnp.float32
BF16 = jnp.bfloat16

D_MODEL = 2048
N_HEADS = 8
HEAD_DIM = 128
V_HEAD_DIM = 2 * HEAD_DIM
QK_WIDTH = 2 * N_HEADS * HEAD_DIM
V_WIDTH = N_HEADS * V_HEAD_DIM
CONV_CH = 1024
CONV_WIDTH = 31
N_BRANCH = 2
FFN_DIM = 5632
FFN_CONV_WIDTH = 3
ROPE_THETA = 10000.0
EPS = 1e-6
IN_WIDTH = 2 * QK_WIDTH + V_WIDTH + 2 * CONV_CH + N_BRANCH * D_MODEL

VMEM_LIMIT_BYTES = 56 * 1024 * 1024
HALO_ROWS = 16
IN_COL_TILE = 1024
VT_ROWS = V_HEAD_DIM + 16
LOG2E = math.log2(math.e)


def _params(*semantics):
    return pltpu.CompilerParams(dimension_semantics=semantics, vmem_limit_bytes=VMEM_LIMIT_BYTES)


def _resident(shape):
    return pl.BlockSpec(shape, lambda *_: (0,) * len(shape), pipeline_mode=pl.Buffered(1))


def _row_tile(seq):
    return min(512, seq)


def _qk_epilogue(z, gain, cos, sin_signed, out_scale):
    outs = []
    for h in range(z.shape[1] // HEAD_DIM):
        zh = z[:, h * HEAD_DIM:(h + 1) * HEAD_DIM]
        ms = jnp.mean(zh * zh, axis=-1, keepdims=True)
        y = zh * lax.rsqrt(ms + EPS) * gain
        y = y * cos + pltpu.roll(y, HEAD_DIM // 2, 1) * sin_signed
        outs.append((y * out_scale).astype(BF16))
    return jnp.concatenate(outs, axis=1)


def _q_epilogue_t(z, gain_col, cos_t, sin_t, out_scale):
    tm = z.shape[0]
    gain = jnp.concatenate([gain_col * out_scale] * (tm // HEAD_DIM), axis=1)
    outs = []
    for h in range(z.shape[1] // (2 * HEAD_DIM)):
        zt = z[:, h * 2 * HEAD_DIM:(h + 1) * 2 * HEAD_DIM].T
        subs = []
        for sub in range(2):
            x = zt[sub * HEAD_DIM:(sub + 1) * HEAD_DIM]
            ms = jnp.mean(x * x, axis=0, keepdims=True)
            y = x * lax.rsqrt(ms + EPS) * gain
            swapped = jnp.concatenate([y[HEAD_DIM // 2:], y[:HEAD_DIM // 2]], axis=0)
            subs.append(y * cos_t + swapped * sin_t)
        outs.append(jnp.concatenate(subs, axis=0).astype(BF16))
    return outs


def _in_proj_kernel(x_ref, g1_ref, w_ref, bg_ref, qg_ref, kg_ref, cos_ref, sin_ref, cost_ref,
                    sint_ref, qt_ref, k_ref, vt_ref, glu_ref, gate_ref, h_ref, ua_ref, *, q_scale):
    j = pl.program_id(1)

    @pl.when(j == 0)
    def _():
        x = x_ref[...]
        ms = jnp.mean(x * x, axis=-1, keepdims=True)
        h_ref[...] = (x * lax.rsqrt(ms + EPS) * g1_ref[...]).astype(BF16)

    z = jnp.dot(h_ref[...], w_ref[...], preferred_element_type=F32)

    @pl.when(j < 2)
    def _():
        heads = _q_epilogue_t(z, qg_ref[...], cost_ref[...], sint_ref[...], q_scale)
        for h, head in enumerate(heads):
            qt_ref[h] = head

    @pl.when((j >= 2) & (j < 4))
    def _():
        k_ref[...] = _qk_epilogue(z, kg_ref[...], cos_ref[...], sin_ref[...], 1.0)

    @pl.when((j >= 4) & (j < 6))
    def _():
        tm = z.shape[0]
        for h in range(z.shape[1] // V_HEAD_DIM):
            vt_ref[h, :V_HEAD_DIM, :] = z[:, h * V_HEAD_DIM:(h + 1) * V_HEAD_DIM].T.astype(BF16)
            vt_ref[h, V_HEAD_DIM:, :] = jnp.ones((VT_ROWS - V_HEAD_DIM, tm), BF16)

    @pl.when(j == 6)
    def _():
        ua_ref[...] = z

    @pl.when(j == 7)
    def _():
        glu_ref[...] = ua_ref[...] * jax.nn.sigmoid(z)

    @pl.when(j >= 8)
    def _():
        gate_ref[...] = jax.nn.sigmoid(z + bg_ref[...]).astype(BF16)


def _in_proj(x2, seq, norm1_g, w_in, b_gate, q_norm_g, k_norm_g, cos, sin_signed):
    rows = x2.shape[0]
    tm = _row_tile(seq)
    tn = IN_COL_TILE
    seq_tiles = seq // tm
    heads_per_step = tn // (2 * HEAD_DIM)
    q_gain_col = jnp.broadcast_to(q_norm_g.reshape(HEAD_DIM, 1), (HEAD_DIM, HEAD_DIM))

    def clamp(j, lo, n):
        return jnp.clip(j - lo, 0, n - 1)

    kernel = functools.partial(_in_proj_kernel, q_scale=HEAD_DIM ** -0.5 * LOG2E)
    out_shapes = (
        jax.ShapeDtypeStruct((rows // tm, N_HEADS, 2 * HEAD_DIM, tm), BF16),
        jax.ShapeDtypeStruct((rows, QK_WIDTH), BF16),
        jax.ShapeDtypeStruct((rows // tm, N_HEADS, VT_ROWS, tm), BF16),
        jax.ShapeDtypeStruct((rows, CONV_CH), F32),
        jax.ShapeDtypeStruct((rows, N_BRANCH * D_MODEL), BF16),
    )
    return pl.pallas_call(
        kernel,
        grid=(rows // tm, IN_WIDTH // tn),
        in_specs=[
            pl.BlockSpec((tm, D_MODEL), lambda i, j: (i, 0)),
            pl.BlockSpec((1, D_MODEL), lambda i, j: (0, 0)),
            pl.BlockSpec((D_MODEL, tn), lambda i, j: (0, j)),
            pl.BlockSpec((1, tn), lambda i, j: (0, clamp(j, 8, 4))),
            pl.BlockSpec((HEAD_DIM, HEAD_DIM), lambda i, j: (0, 0)),
            pl.BlockSpec((1, HEAD_DIM), lambda i, j: (0, 0)),
            pl.BlockSpec((tm, HEAD_DIM), lambda i, j: (i % seq_tiles, 0)),
            pl.BlockSpec((tm, HEAD_DIM), lambda i, j: (i % seq_tiles, 0)),
            pl.BlockSpec((HEAD_DIM, tm), lambda i, j: (0, i % seq_tiles)),
            pl.BlockSpec((HEAD_DIM, tm), lambda i, j: (0, i % seq_tiles)),
        ],
        out_specs=(
            pl.BlockSpec((None, heads_per_step, 2 * HEAD_DIM, tm),
                         lambda i, j: (i, clamp(j, 0, 2), 0, 0)),
            pl.BlockSpec((tm, tn), lambda i, j: (i, clamp(j, 2, 2))),
            pl.BlockSpec((None, heads_per_step, VT_ROWS, tm),
                         lambda i, j: (i, clamp(j, 4, 2), 0, 0)),
            pl.BlockSpec((tm, CONV_CH), lambda i, j: (i, 0)),
            pl.BlockSpec((tm, tn), lambda i, j: (i, clamp(j, 8, 4))),
        ),
        out_shape=out_shapes,
        scratch_shapes=[pltpu.VMEM((tm, D_MODEL), BF16), pltpu.VMEM((tm, CONV_CH), F32)],
        compiler_params=_params("parallel", "arbitrary"),
        name="in_proj",
    )(x2, norm1_g, w_in, b_gate, q_gain_col, k_norm_g, cos, sin_signed, cos.T, sin_signed.T)


def _attn_kernel(lq1_ref, lk1_ref, lq2_ref, lk2_ref, sg_ref, qt_ref, k_ref, vt_ref, o_ref,
                 s0_ref, s1_ref, x0_ref, x1_ref, p0_ref, p1_ref, a0_ref, a1_ref, m_ref, acc_ref,
                 *, tq, k_chunk, lam_init):
    n_chunks = k_ref.shape[0] // k_chunk
    n_units = qt_ref.shape[0] * n_chunks
    s_bufs, p_bufs = (s0_ref, s1_ref), (p0_ref, p1_ref)
    a_bufs, max_bufs = (a0_ref, a1_ref), (x0_ref, x1_ref)

    def scores(u, slot):
        c0 = pl.multiple_of((u % n_chunks) * k_chunk, k_chunk)
        for sub in range(2):
            dims = slice(sub * HEAD_DIM, (sub + 1) * HEAD_DIM)
            st = jnp.dot(k_ref[pl.ds(c0, k_chunk), dims], qt_ref[u // n_chunks, dims, :],
                         preferred_element_type=F32)
            s_bufs[slot][sub] = st
            max_bufs[slot][sub] = jnp.broadcast_to(jnp.max(st, axis=0, keepdims=True), (8, tq))

    def softmax(u, slot):
        m_old = m_ref[...]
        if slot == 0:
            m_old = m_old + jnp.where(u % n_chunks == 0, -jnp.inf, 0.0)
        m_new = jnp.maximum(m_old, max_bufs[slot][...])
        a_bufs[slot][...] = jnp.exp2(m_old - m_new)
        m_ref[...] = m_new
        for sub in range(2):
            d = s_bufs[slot][sub] - m_new[sub, 0:1, :]
            p_bufs[slot][sub] = jnp.exp2(d).astype(BF16)

    def values(u, slot):
        vt = vt_ref[u % n_chunks]
        for sub in range(2):
            pv = jnp.dot(vt, p_bufs[slot][sub], preferred_element_type=F32)
            acc_ref[sub] = a_bufs[slot][sub, 0:1, :] * acc_ref[sub] + pv

    def normalise(tile):
        lam = (jnp.exp(jnp.sum(lq1_ref[...] * lk1_ref[...], axis=-1, keepdims=True))
               - jnp.exp(jnp.sum(lq2_ref[...] * lk2_ref[...], axis=-1, keepdims=True)) + lam_init)
        o0 = acc_ref[0, :V_HEAD_DIM, :] / acc_ref[0, V_HEAD_DIM:V_HEAD_DIM + 1, :]
        o1 = acc_ref[1, :V_HEAD_DIM, :] / acc_ref[1, V_HEAD_DIM:V_HEAD_DIM + 1, :]
        o = o0 - lam * o1
        ms = jnp.mean(o * o, axis=0, keepdims=True)
        gain = jnp.concatenate([sg_ref[...]] * (tq // HEAD_DIM), axis=1)
        o = o * lax.rsqrt(ms + EPS) * gain * (1.0 - lam_init)
        o_ref[pl.ds(pl.multiple_of(tile * tq, tq), tq), :] = o.T.astype(BF16)

    m_ref[...] = jnp.full(m_ref.shape, -jnp.inf, F32)
    acc_ref[...] = jnp.zeros(acc_ref.shape, F32)

    scores(0, 0)
    scores(1, 1)
    softmax(0, 0)

    def body(t, carry):
        u = 2 * t + 2
        scores(u, 0)
        softmax(u - 1, 1)
        values(u - 2, 0)
        scores(u + 1, 1)
        softmax(u, 0)
        values(u - 1, 1)

        @pl.when(u % n_chunks == 0)
        def _():
            normalise(u // n_chunks - 1)

        return carry

    lax.fori_loop(0, n_units // 2 - 1, body, 0)
    softmax(n_units - 1, 1)
    values(n_units - 2, 0)
    values(n_units - 1, 1)
    normalise(n_units // n_chunks - 1)


def _attention(qt, k, vt, batch, seq, lq1, lk1, lq2, lk2, subln_g, lam_init):
    tq = qt.shape[-1]
    k_chunk = vt.shape[-1]
    tq_group = min(2048, seq)
    groups = seq // tq_group
    assert (seq // k_chunk) % 2 == 0
    gain = jnp.broadcast_to(subln_g.reshape(V_HEAD_DIM, 1), (V_HEAD_DIM, HEAD_DIM))
    vec = pl.BlockSpec((1, HEAD_DIM), lambda b, h, i: (0, 0))
    kernel = functools.partial(_attn_kernel, tq=tq, k_chunk=k_chunk, lam_init=lam_init)
    s_buf = pltpu.VMEM((2, k_chunk, tq), F32)
    p_buf = pltpu.VMEM((2, k_chunk, tq), BF16)
    stat = pltpu.VMEM((2, 8, tq), F32)
    return pl.pallas_call(
        kernel,
        grid=(batch, N_HEADS, groups),
        in_specs=[
            vec, vec, vec, vec,
            pl.BlockSpec((V_HEAD_DIM, HEAD_DIM), lambda b, h, i: (0, 0)),
            pl.BlockSpec((tq_group // tq, None, 2 * HEAD_DIM, tq),
                         lambda b, h, i: (b * groups + i, h, 0, 0)),
            pl.BlockSpec((None, seq, 2 * HEAD_DIM), lambda b, h, i: (b, 0, h),
                         pipeline_mode=pl.Buffered(1)),
            pl.BlockSpec((seq // k_chunk, None, VT_ROWS, k_chunk), lambda b, h, i: (b, h, 0, 0),
                         pipeline_mode=pl.Buffered(1)),
        ],
        out_specs=pl.BlockSpec((None, tq_group, V_HEAD_DIM), lambda b, h, i: (b, i, h)),
        out_shape=jax.ShapeDtypeStruct((batch, seq, V_WIDTH), BF16),
        scratch_shapes=[s_buf, s_buf, stat, stat, p_buf, p_buf, stat, stat, stat,
                        pltpu.VMEM((2, VT_ROWS, tq), F32)],
        compiler_params=_params("parallel", "parallel", "arbitrary"),
        name="diff_attention",
    )(lq1, lk1, lq2, lk2, gain, qt, k, vt)


def _conv_kernel(prev_ref, cur_ref, next_ref, w_ref, b_ref, lg_ref, lb_ref, o_ref, ext_ref,
                 *, row_chunk, norm_chunk):
    i = pl.program_id(1)
    tm = cur_ref.shape[0]
    pad = (CONV_WIDTH - 1) // 2
    ext_ref[0:HALO_ROWS] = jnp.where(i > 0, prev_ref[...], 0.0)
    ext_ref[HALO_ROWS:HALO_ROWS + tm] = cur_ref[...]
    ext_ref[HALO_ROWS + tm:] = jnp.where(i < pl.num_programs(1) - 1, next_ref[...], 0.0)
    bias, ln_g, ln_b = b_ref[...], lg_ref[...], lb_ref[...]

    def channel_sum(x):
        return jnp.sum(jnp.sum(x, axis=2, keepdims=True), axis=1, keepdims=True)

    def conv_body(r, carry):
        r0 = r * row_chunk
        acc = jnp.broadcast_to(bias, (row_chunk,) + bias.shape)
        for t in range(CONV_WIDTH):
            off = HALO_ROWS - pad + t
            acc = acc + w_ref[t] * ext_ref[pl.ds(r0 + off, row_chunk)]
        o_ref[pl.ds(r0, row_chunk)] = acc
        return carry

    lax.fori_loop(0, tm // row_chunk, conv_body, 0)

    def norm_body(r, carry):
        rows = pl.ds(r * norm_chunk, norm_chunk)
        y = o_ref[rows]
        xc = y - channel_sum(y) * (1.0 / CONV_CH)
        var = channel_sum(xc * xc) * (1.0 / CONV_CH)
        y = xc * lax.rsqrt(var + EPS) * ln_g + ln_b
        o_ref[rows] = y * jax.nn.sigmoid(y)
        return carry

    lax.fori_loop(0, tm // norm_chunk, norm_body, 0)


def _halo_specs(tm, seq, width, col_map):
    per_tile = tm // HALO_ROWS
    last = seq // HALO_ROWS - 1
    prev = pl.BlockSpec((None, HALO_ROWS, width),
                        lambda b, i, *f: (b, jnp.maximum(i * per_tile - 1, 0), col_map(*f)))
    nxt = pl.BlockSpec((None, HALO_ROWS, width),
                       lambda b, i, *f: (b, jnp.minimum((i + 1) * per_tile, last), col_map(*f)))
    return prev, nxt


def _conv_module(glu, conv_dw_w, conv_dw_b, conv_ln_g, conv_ln_b):
    batch, seq, _ = glu.shape
    tm = _row_tile(seq)
    tile = (CONV_CH // 128, 128)
    per_tile = tm // HALO_ROWS
    last = seq // HALO_ROWS - 1
    vec = pl.BlockSpec(tile, lambda b, i: (0, 0))
    glu4 = glu.reshape((batch, seq) + tile)
    out = pl.pallas_call(
        functools.partial(_conv_kernel, row_chunk=16, norm_chunk=64),
        grid=(batch, seq // tm),
        in_specs=[
            pl.BlockSpec((None, HALO_ROWS) + tile,
                         lambda b, i: (b, jnp.maximum(i * per_tile - 1, 0), 0, 0)),
            pl.BlockSpec((None, tm) + tile, lambda b, i: (b, i, 0, 0)),
            pl.BlockSpec((None, HALO_ROWS) + tile,
                         lambda b, i: (b, jnp.minimum((i + 1) * per_tile, last), 0, 0)),
            pl.BlockSpec((CONV_WIDTH,) + tile, lambda b, i: (0, 0, 0)),
            vec, vec, vec,
        ],
        out_specs=pl.BlockSpec((None, tm) + tile, lambda b, i: (b, i, 0, 0)),
        out_shape=jax.ShapeDtypeStruct((batch, seq) + tile, F32),
        scratch_shapes=[pltpu.VMEM((tm + 2 * HALO_ROWS,) + tile, F32)],
        compiler_params=_params("parallel", "parallel"),
        name="conv_module",
    )(glu4, glu4, glu4, conv_dw_w.reshape((CONV_WIDTH,) + tile), conv_dw_b.reshape(tile),
      conv_ln_g.reshape(tile), conv_ln_b.reshape(tile))
    return out.reshape(batch, seq, CONV_CH)


def _merge_kernel(o_ref, c_ref, g_ref, wa_ref, wc_ref, m_ref):
    a = jnp.dot(o_ref[...], wa_ref[...], preferred_element_type=F32)
    cb = jnp.dot(c_ref[...].astype(BF16), wc_ref[...], preferred_element_type=F32)
    m = g_ref[:, :D_MODEL].astype(F32) * a + g_ref[:, D_MODEL:].astype(F32) * cb
    m_ref[...] = m.astype(BF16)


def _merge(o2, c2, gates, w_attn_proj, w_conv_proj, tm):
    rows = o2.shape[0]
    return pl.pallas_call(
        _merge_kernel,
        grid=(rows // tm,),
        in_specs=[
            pl.BlockSpec((tm, V_WIDTH), lambda i: (i, 0)),
            pl.BlockSpec((tm, CONV_CH), lambda i: (i, 0)),
            pl.BlockSpec((tm, N_BRANCH * D_MODEL), lambda i: (i, 0)),
            _resident((V_WIDTH, D_MODEL)),
            _resident((CONV_CH, D_MODEL)),
        ],
        out_specs=pl.BlockSpec((tm, D_MODEL), lambda i: (i, 0)),
        out_shape=jax.ShapeDtypeStruct((rows, D_MODEL), BF16),
        compiler_params=_params("parallel"),
        name="merge",
    )(o2, c2, gates, w_attn_proj, w_conv_proj)


def _out_proj_kernel(x_ref, m_ref, w_ref, y_ref):
    y_ref[...] = x_ref[...] + jnp.dot(m_ref[...], w_ref[...], preferred_element_type=F32)


def _out_proj(x2, m2, w_out, tm):
    rows = x2.shape[0]
    return pl.pallas_call(
        _out_proj_kernel,
        grid=(rows // tm,),
        in_specs=[
            pl.BlockSpec((tm, D_MODEL), lambda i: (i, 0)),
            pl.BlockSpec((tm, D_MODEL), lambda i: (i, 0)),
            _resident((D_MODEL, D_MODEL)),
        ],
        out_specs=pl.BlockSpec((tm, D_MODEL), lambda i: (i, 0)),
        out_shape=jax.ShapeDtypeStruct((rows, D_MODEL), F32),
        compiler_params=_params("parallel"),
        name="out_proj",
    )(x2, m2, w_out)


def _ffn_up_kernel(x_ref, g_ref, w_ref, up_ref, h_ref):
    @pl.when(pl.program_id(1) == 0)
    def _():
        x = x_ref[...]
        ms = jnp.mean(x * x, axis=-1, keepdims=True)
        h_ref[...] = (x * lax.rsqrt(ms + EPS) * g_ref[...]).astype(BF16)

    up_ref[...] = jnp.dot(h_ref[...], w_ref[...], preferred_element_type=F32).astype(BF16)


def _ffn_up(x2, norm2_g, w_up, tm):
    rows = x2.shape[0]
    tn = 1408
    return pl.pallas_call(
        _ffn_up_kernel,
        grid=(rows // tm, 2 * FFN_DIM // tn),
        in_specs=[
            pl.BlockSpec((tm, D_MODEL), lambda i, j: (i, 0)),
            pl.BlockSpec((1, D_MODEL), lambda i, j: (0, 0)),
            pl.BlockSpec((D_MODEL, tn), lambda i, j: (0, j)),
        ],
        out_specs=pl.BlockSpec((tm, tn), lambda i, j: (i, j)),
        out_shape=jax.ShapeDtypeStruct((rows, 2 * FFN_DIM), BF16),
        scratch_shapes=[pltpu.VMEM((tm, D_MODEL), BF16)],
        compiler_params=_params("parallel", "arbitrary"),
        name="ffn_up",
    )(x2, norm2_g, w_up)


def _conv3(prev_ref, cur_ref, next_ref, w_ref, b_ref, first, last):
    cur = cur_ref[...].astype(F32)
    tm = cur.shape[0]
    row = lax.broadcasted_iota(jnp.int32, cur.shape, 0)
    before = jnp.where(first, 0.0, prev_ref[HALO_ROWS - 1:HALO_ROWS, :].astype(F32))
    after = jnp.where(last, 0.0, next_ref[0:1, :].astype(F32))
    up_m1 = jnp.where(row == 0, before, pltpu.roll(cur, 1, 0))
    up_p1 = jnp.where(row == tm - 1, after, pltpu.roll(cur, tm - 1, 0))
    return w_ref[0:1, :] * up_m1 + w_ref[1:2, :] * cur + w_ref[2:3, :] * up_p1 + b_ref[...]


def _ffn_down_kernel(pa_ref, a_ref, na_ref, pb_ref, b_ref, nb_ref, wa_ref, ba_ref, wb_ref, bb_ref,
                     wd_ref, x_ref, y_ref):
    i = pl.program_id(1)
    f = pl.program_id(2)
    first = i == 0
    last = i == pl.num_programs(1) - 1
    fa = _conv3(pa_ref, a_ref, na_ref, wa_ref, ba_ref, first, last)
    fb = _conv3(pb_ref, b_ref, nb_ref, wb_ref, bb_ref, first, last)
    act = (fa * jax.nn.sigmoid(fa) * fb).astype(BF16)
    contrib = jnp.dot(act, wd_ref[...], preferred_element_type=F32)

    @pl.when(f == 0)
    def _():
        y_ref[...] = x_ref[...] + contrib

    @pl.when(f > 0)
    def _():
        y_ref[...] += contrib


def _ffn_down(up, x1, ffn_dw_w, ffn_dw_b, w_down):
    batch, seq, _ = up.shape
    tm = _row_tile(seq)
    tf = 512
    nf = FFN_DIM // tf
    prev_a, next_a = _halo_specs(tm, seq, tf, lambda f: f)
    prev_b, next_b = _halo_specs(tm, seq, tf, lambda f: f + nf)
    return pl.pallas_call(
        _ffn_down_kernel,
        grid=(batch, seq // tm, nf),
        in_specs=[
            prev_a, pl.BlockSpec((None, tm, tf), lambda b, i, f: (b, i, f)), next_a,
            prev_b, pl.BlockSpec((None, tm, tf), lambda b, i, f: (b, i, f + nf)), next_b,
            pl.BlockSpec((FFN_CONV_WIDTH, tf), lambda b, i, f: (0, f)),
            pl.BlockSpec((1, tf), lambda b, i, f: (0, f)),
            pl.BlockSpec((FFN_CONV_WIDTH, tf), lambda b, i, f: (0, f + nf)),
            pl.BlockSpec((1, tf), lambda b, i, f: (0, f + nf)),
            pl.BlockSpec((tf, D_MODEL), lambda b, i, f: (f, 0)),
            pl.BlockSpec((None, tm, D_MODEL), lambda b, i, f: (b, i, 0)),
        ],
        out_specs=pl.BlockSpec((None, tm, D_MODEL), lambda b, i, f: (b, i, 0)),
        out_shape=jax.ShapeDtypeStruct((batch, seq, D_MODEL), F32),
        compiler_params=_params("parallel", "parallel", "arbitrary"),
        name="ffn_down",
    )(up, up, up, up, up, up, ffn_dw_w, ffn_dw_b, ffn_dw_w, ffn_dw_b, w_down, x1)


def _rope_tables(seq):
    inv_freq = 1.0 / (ROPE_THETA ** (jnp.arange(0, HEAD_DIM, 2, dtype=F32) / HEAD_DIM))
    ang = jnp.arange(seq, dtype=F32)[:, None] * inv_freq[None, :]
    ang = jnp.concatenate([ang, ang], axis=-1)
    sign = jnp.where(jnp.arange(HEAD_DIM) < HEAD_DIM // 2, -1.0, 1.0).astype(F32)
    return jnp.cos(ang), jnp.sin(ang) * sign


def _encoder_layer(x, layer_idx, p):
    batch, seq, _ = x.shape
    rows = batch * seq
    tm = _row_tile(seq)
    lam_init = 0.8 - 0.6 * math.exp(-0.3 * layer_idx)
    cos, sin_signed = _rope_tables(seq)
    x2 = x.reshape(rows, D_MODEL)

    qt, k, vt, glu, gates = _in_proj(x2, seq, p["norm1_g"], p["w_in"], p["b_gate"], p["q_norm_g"],
                                     p["k_norm_g"], cos, sin_signed)
    o = _attention(qt, k.reshape(batch, seq, QK_WIDTH), vt, batch, seq, p["lambda_q1"],
                   p["lambda_k1"], p["lambda_q2"], p["lambda_k2"], p["subln_g"], lam_init)
    c = _conv_module(glu.reshape(batch, seq, CONV_CH), p["conv_dw_w"], p["conv_dw_b"],
                     p["conv_ln_g"], p["conv_ln_b"])
    m = _merge(o.reshape(rows, V_WIDTH), c.reshape(rows, CONV_CH), gates, p["w_attn_proj"],
               p["w_conv_proj"], tm)
    x1 = _out_proj(x2, m, p["w_out"], tm)
    up = _ffn_up(x1, p["norm2_g"], p["w_up"], tm)
    y = _ffn_down(up.reshape(batch, seq, 2 * FFN_DIM), x1.reshape(batch, seq, D_MODEL),
                  p["ffn_dw_w"], p["ffn_dw_b"], p["w_down"])
    return y


_MATRICES = ("w_in", "w_attn_proj", "w_conv_proj", "w_out", "w_up", "w_down")
_ROW_VECTORS = ("norm1_g", "b_gate", "q_norm_g", "k_norm_g", "lambda_q1", "lambda_k1", "lambda_q2",
                "lambda_k2", "subln_g", "conv_dw_b", "conv_ln_g", "conv_ln_b", "norm2_g", "ffn_dw_b")


def kernel(x_prompt, x_sample, norm1_g, w_in, b_gate, q_norm_g, k_norm_g, lambda_q1, lambda_k1, lambda_q2, lambda_k2, subln_g, w_attn_proj, conv_dw_w, conv_dw_b, conv_ln_g, conv_ln_b, w_conv_proj, w_out, norm2_g, w_up, ffn_dw_w, ffn_dw_b, w_down):
    stacked = dict(norm1_g=norm1_g, w_in=w_in, b_gate=b_gate, q_norm_g=q_norm_g, k_norm_g=k_norm_g,
                   lambda_q1=lambda_q1, lambda_k1=lambda_k1, lambda_q2=lambda_q2,
                   lambda_k2=lambda_k2, subln_g=subln_g, w_attn_proj=w_attn_proj,
                   conv_dw_w=conv_dw_w, conv_dw_b=conv_dw_b, conv_ln_g=conv_ln_g,
                   conv_ln_b=conv_ln_b, w_conv_proj=w_conv_proj, w_out=w_out, norm2_g=norm2_g,
                   w_up=w_up, ffn_dw_w=ffn_dw_w, ffn_dw_b=ffn_dw_b, w_down=w_down)
    depth = w_in.shape[0]
    layers = []
    for l in range(depth):
        p = {}
        for name, value in stacked.items():
            value = value[l]
            if name in _MATRICES:
                value = value.astype(BF16)
            elif name in _ROW_VECTORS:
                value = value.astype(F32)[None, :]
            else:
                value = value.astype(F32)
            p[name] = value
        layers.append(p)

    def run(x):
        for l, p in enumerate(layers):
            x = _encoder_layer(x, l, p)
        return x

    return run(x_prompt), run(x_sample)
```

```python
import functools
import math

import jax
import jax.numpy as jnp
from jax import lax
from jax.experimental import pallas as pl
from jax.experimental.pallas import tpu as pltpu

F32 = jnp.float32
BF16 = jnp.bfloat16

D_MODEL = 2048
N_HEADS = 8
HEAD_DIM = 128
V_HEAD_DIM = 2 * HEAD_DIM
QK_WIDTH = 2 * N_HEADS * HEAD_DIM
V_WIDTH = N_HEADS * V_HEAD_DIM
CONV_CH = 1024
CONV_WIDTH = 31
N_BRANCH = 2
FFN_DIM = 5632
FFN_CONV_WIDTH = 3
ROPE_THETA = 10000.0
EPS = 1e-6
IN_WIDTH = 2 * QK_WIDTH + V_WIDTH + 2 * CONV_CH + N_BRANCH * D_MODEL

VMEM_LIMIT_BYTES = 56 * 1024 * 1024
HALO_ROWS = 16
IN_COL_TILE = 1024
VT_ROWS = V_HEAD_DIM + 16
LOG2E = math.log2(math.e)


def _params(*semantics):
    return pltpu.CompilerParams(dimension_semantics=semantics, vmem_limit_bytes=VMEM_LIMIT_BYTES)


def _resident(shape):
    return pl.BlockSpec(shape, lambda *_: (0,) * len(shape), pipeline_mode=pl.Buffered(1))


def _row_tile(seq):
    return min(512, seq)


def _sigmoid(x):
    return 0.5 * jnp.tanh(0.5 * x) + 0.5


def _qk_epilogue(z, gain, cos, sin_signed, out_scale):
    outs = []
    for h in range(z.shape[1] // HEAD_DIM):
        zh = z[:, h * HEAD_DIM:(h + 1) * HEAD_DIM]
        ms = jnp.mean(zh * zh, axis=-1, keepdims=True)
        y = zh * lax.rsqrt(ms + EPS) * gain
        y = y * cos + pltpu.roll(y, HEAD_DIM // 2, 1) * sin_signed
        outs.append((y * out_scale).astype(BF16))
    return jnp.concatenate(outs, axis=1)


def _q_epilogue_t(z, gain_col, cos_t, sin_t, out_scale):
    tm = z.shape[0]
    gain = jnp.concatenate([gain_col * out_scale] * (tm // HEAD_DIM), axis=1)
    outs = []
    for h in range(z.shape[1] // (2 * HEAD_DIM)):
        zt = z[:, h * 2 * HEAD_DIM:(h + 1) * 2 * HEAD_DIM].T
        subs = []
        for sub in range(2):
            x = zt[sub * HEAD_DIM:(sub + 1) * HEAD_DIM]
            ms = jnp.mean(x * x, axis=0, keepdims=True)
            y = x * lax.rsqrt(ms + EPS) * gain
            swapped = jnp.concatenate([y[HEAD_DIM // 2:], y[:HEAD_DIM // 2]], axis=0)
            subs.append(y * cos_t + swapped * sin_t)
        outs.append(jnp.concatenate(subs, axis=0).astype(BF16))
    return outs


def _in_proj_kernel(x_ref, g1_ref, w_ref, bg_ref, qg_ref, kg_ref, cos_ref, sin_ref, cost_ref,
                    sint_ref, qt_ref, k_ref, vt_ref, glu_ref, gate_ref, h_ref, ua_ref, *, q_scale):
    j = pl.program_id(1)

    @pl.when(j == 0)
    def _():
        x = x_ref[...]
        ms = jnp.mean(x * x, axis=-1, keepdims=True)
        h_ref[...] = (x * lax.rsqrt(ms + EPS) * g1_ref[...]).astype(BF16)

    z = jnp.dot(h_ref[...], w_ref[...], preferred_element_type=F32)

    @pl.when(j < 2)
    def _():
        heads = _q_epilogue_t(z, qg_ref[...], cost_ref[...], sint_ref[...], q_scale)
        for h, head in enumerate(heads):
            qt_ref[h] = head

    @pl.when((j >= 2) & (j < 4))
    def _():
        k_ref[...] = _qk_epilogue(z, kg_ref[...], cos_ref[...], sin_ref[...], 1.0)

    @pl.when((j >= 4) & (j < 6))
    def _():
        tm = z.shape[0]
        for h in range(z.shape[1] // V_HEAD_DIM):
            vt_ref[h, :V_HEAD_DIM, :] = z[:, h * V_HEAD_DIM:(h + 1) * V_HEAD_DIM].T.astype(BF16)
            vt_ref[h, V_HEAD_DIM:, :] = jnp.ones((VT_ROWS - V_HEAD_DIM, tm), BF16)

    @pl.when(j == 6)
    def _():
        ua_ref[...] = z

    @pl.when(j == 7)
    def _():
        glu_ref[...] = ua_ref[...] * _sigmoid(z)

    @pl.when(j >= 8)
    def _():
        gate_ref[...] = _sigmoid(z + bg_ref[...]).astype(BF16)


def _in_proj(x2, seq, norm1_g, w_in, b_gate, q_norm_g, k_norm_g, cos, sin_signed):
    rows = x2.shape[0]
    tm = _row_tile(seq)
    tn = IN_COL_TILE
    seq_tiles = seq // tm
    heads_per_step = tn // (2 * HEAD_DIM)
    q_gain_col = jnp.broadcast_to(q_norm_g.reshape(HEAD_DIM, 1), (HEAD_DIM, HEAD_DIM))

    def clamp(j, lo, n):
        return jnp.clip(j - lo, 0, n - 1)

    kernel = functools.partial(_in_proj_kernel, q_scale=HEAD_DIM ** -0.5 * LOG2E)
    out_shapes = (
        jax.ShapeDtypeStruct((rows // tm, N_HEADS, 2 * HEAD_DIM, tm), BF16),
        jax.ShapeDtypeStruct((rows, QK_WIDTH), BF16),
        jax.ShapeDtypeStruct((rows // tm, N_HEADS, VT_ROWS, tm), BF16),
        jax.ShapeDtypeStruct((rows, CONV_CH), F32),
        jax.ShapeDtypeStruct((rows, N_BRANCH * D_MODEL), BF16),
    )
    return pl.pallas_call(
        kernel,
        grid=(rows // tm, IN_WIDTH // tn),
        in_specs=[
            pl.BlockSpec((tm, D_MODEL), lambda i, j: (i, 0)),
            pl.BlockSpec((1, D_MODEL), lambda i, j: (0, 0)),
            pl.BlockSpec((D_MODEL, tn), lambda i, j: (0, j)),
            pl.BlockSpec((1, tn), lambda i, j: (0, clamp(j, 8, 4))),
            pl.BlockSpec((HEAD_DIM, HEAD_DIM), lambda i, j: (0, 0)),
            pl.BlockSpec((1, HEAD_DIM), lambda i, j: (0, 0)),
            pl.BlockSpec((tm, HEAD_DIM), lambda i, j: (i % seq_tiles, 0)),
            pl.BlockSpec((tm, HEAD_DIM), lambda i, j: (i % seq_tiles, 0)),
            pl.BlockSpec((HEAD_DIM, tm), lambda i, j: (0, i % seq_tiles)),
            pl.BlockSpec((HEAD_DIM, tm), lambda i, j: (0, i % seq_tiles)),
        ],
        out_specs=(
            pl.BlockSpec((None, heads_per_step, 2 * HEAD_DIM, tm),
                         lambda i, j: (i, clamp(j, 0, 2), 0, 0)),
            pl.BlockSpec((tm, tn), lambda i, j: (i, clamp(j, 2, 2))),
            pl.BlockSpec((None, heads_per_step, VT_ROWS, tm),
                         lambda i, j: (i, clamp(j, 4, 2), 0, 0)),
            pl.BlockSpec((tm, CONV_CH), lambda i, j: (i, 0)),
            pl.BlockSpec((tm, tn), lambda i, j: (i, clamp(j, 8, 4))),
        ),
        out_shape=out_shapes,
        scratch_shapes=[pltpu.VMEM((tm, D_MODEL), BF16), pltpu.VMEM((tm, CONV_CH), F32)],
        compiler_params=_params("parallel", "arbitrary"),
        name="in_proj",
    )(x2, norm1_g, w_in, b_gate, q_gain_col, k_norm_g, cos, sin_signed, cos.T, sin_signed.T)


def _attn_kernel(lq1_ref, lk1_ref, lq2_ref, lk2_ref, sg_ref, qt_ref, k_ref, vt_ref, o_ref,
                 s0_ref, s1_ref, x0_ref, x1_ref, p0_ref, p1_ref, a0_ref, a1_ref, m_ref, acc_ref,
                 *, tq, k_chunk, lam_init):
    n_chunks = k_ref.shape[0] // k_chunk
    n_units = qt_ref.shape[0] * n_chunks
    s_bufs, p_bufs = (s0_ref, s1_ref), (p0_ref, p1_ref)
    a_bufs, max_bufs = (a0_ref, a1_ref), (x0_ref, x1_ref)

    def scores(u, slot):
        c0 = pl.multiple_of((u % n_chunks) * k_chunk, k_chunk)
        for sub in range(2):
            dims = slice(sub * HEAD_DIM, (sub + 1) * HEAD_DIM)
            st = jnp.dot(k_ref[pl.ds(c0, k_chunk), dims], qt_ref[u // n_chunks, dims, :],
                         preferred_element_type=F32)
            s_bufs[slot][sub] = st
            max_bufs[slot][sub] = jnp.broadcast_to(jnp.max(st, axis=0, keepdims=True), (8, tq))

    def softmax(u, slot):
        m_old = m_ref[...]
        if slot == 0:
            m_old = m_old + jnp.where(u % n_chunks == 0, -jnp.inf, 0.0)
        m_new = jnp.maximum(m_old, max_bufs[slot][...])
        a_bufs[slot][...] = jnp.exp2(m_old - m_new)
        m_ref[...] = m_new
        for sub in range(2):
            d = s_bufs[slot][sub] - m_new[sub, 0:1, :]
            p_bufs[slot][sub] = jnp.exp2(d).astype(BF16)

    def values(u, slot):
        vt = vt_ref[u % n_chunks]
        for sub in range(2):
            pv = jnp.dot(vt, p_bufs[slot][sub], preferred_element_type=F32)
            acc_ref[sub] = a_bufs[slot][sub, 0:1, :] * acc_ref[sub] + pv

    def normalise(tile):
        lam = (jnp.exp(jnp.sum(lq1_ref[...] * lk1_ref[...], axis=-1, keepdims=True))
               - jnp.exp(jnp.sum(lq2_ref[...] * lk2_ref[...], axis=-1, keepdims=True)) + lam_init)
        o0 = acc_ref[0, :V_HEAD_DIM, :] / acc_ref[0, V_HEAD_DIM:V_HEAD_DIM + 1, :]
        o1 = acc_ref[1, :V_HEAD_DIM, :] / acc_ref[1, V_HEAD_DIM:V_HEAD_DIM + 1, :]
        o = o0 - lam * o1
        ms = jnp.mean(o * o, axis=0, keepdims=True)
        gain = jnp.concatenate([sg_ref[...]] * (tq // HEAD_DIM), axis=1)
        o = o * lax.rsqrt(ms + EPS) * gain * (1.0 - lam_init)
        o_ref[pl.ds(pl.multiple_of(tile * tq, tq), tq), :] = o.T.astype(BF16)

    m_ref[...] = jnp.full(m_ref.shape, -jnp.inf, F32)
    acc_ref[...] = jnp.zeros(acc_ref.shape, F32)

    scores(0, 0)
    scores(1, 1)
    softmax(0, 0)

    def body(t, carry):
        u = 2 * t + 2
        scores(u, 0)
        softmax(u - 1, 1)
        values(u - 2, 0)
        scores(u + 1, 1)
        softmax(u, 0)
        values(u - 1, 1)

        @pl.when(u % n_chunks == 0)
        def _():
            normalise(u // n_chunks - 1)

        return carry

    lax.fori_loop(0, n_units // 2 - 1, body, 0)
    softmax(n_units - 1, 1)
    values(n_units - 2, 0)
    values(n_units - 1, 1)
    normalise(n_units // n_chunks - 1)


def _attention(qt, k, vt, batch, seq, lq1, lk1, lq2, lk2, subln_g, lam_init):
    tq = qt.shape[-1]
    k_chunk = vt.shape[-1]
    tq_group = min(2048, seq)
    groups = seq // tq_group
    assert (seq // k_chunk) % 2 == 0
    gain = jnp.broadcast_to(subln_g.reshape(V_HEAD_DIM, 1), (V_HEAD_DIM, HEAD_DIM))
    vec = pl.BlockSpec((1, HEAD_DIM), lambda b, h, i: (0, 0))
    kernel = functools.partial(_attn_kernel, tq=tq, k_chunk=k_chunk, lam_init=lam_init)
    s_buf = pltpu.VMEM((2, k_chunk, tq), F32)
    p_buf = pltpu.VMEM((2, k_chunk, tq), BF16)
    stat = pltpu.VMEM((2, 8, tq), F32)
    return pl.pallas_call(
        kernel,
        grid=(batch, N_HEADS, groups),
        in_specs=[
            vec, vec, vec, vec,
            pl.BlockSpec((V_HEAD_DIM, HEAD_DIM), lambda b, h, i: (0, 0)),
            pl.BlockSpec((tq_group // tq, None, 2 * HEAD_DIM, tq),
                         lambda b, h, i: (b * groups + i, h, 0, 0)),
            pl.BlockSpec((None, seq, 2 * HEAD_DIM), lambda b, h, i: (b, 0, h)),
            pl.BlockSpec((seq // k_chunk, None, VT_ROWS, k_chunk), lambda b, h, i: (b, h, 0, 0)),
        ],
        out_specs=pl.BlockSpec((None, tq_group, V_HEAD_DIM), lambda b, h, i: (b, i, h)),
        out_shape=jax.ShapeDtypeStruct((batch, seq, V_WIDTH), BF16),
        scratch_shapes=[s_buf, s_buf, stat, stat, p_buf, p_buf, stat, stat, stat,
                        pltpu.VMEM((2, VT_ROWS, tq), F32)],
        compiler_params=_params("parallel", "parallel", "arbitrary"),
        name="diff_attention",
    )(lq1, lk1, lq2, lk2, gain, qt, k, vt)


def _conv_kernel(prev_ref, cur_ref, next_ref, w_ref, b_ref, lg_ref, lb_ref, o_ref, ext_ref,
                 *, row_chunk, norm_chunk):
    i = pl.program_id(1)
    tm = cur_ref.shape[0]
    pad = (CONV_WIDTH - 1) // 2
    ext_ref[0:HALO_ROWS] = jnp.where(i > 0, prev_ref[...], 0.0)
    ext_ref[HALO_ROWS:HALO_ROWS + tm] = cur_ref[...]
    ext_ref[HALO_ROWS + tm:] = jnp.where(i < pl.num_programs(1) - 1, next_ref[...], 0.0)
    bias, ln_g, ln_b = b_ref[...], lg_ref[...], lb_ref[...]

    def channel_sum(x):
        return jnp.sum(jnp.sum(x, axis=2, keepdims=True), axis=1, keepdims=True)

    def conv_body(r, carry):
        r0 = r * row_chunk
        acc = jnp.broadcast_to(bias, (row_chunk,) + bias.shape)
        for t in range(CONV_WIDTH):
            off = HALO_ROWS - pad + t
            acc = acc + w_ref[t] * ext_ref[pl.ds(r0 + off, row_chunk)]
        o_ref[pl.ds(r0, row_chunk)] = acc
        return carry

    lax.fori_loop(0, tm // row_chunk, conv_body, 0)

    def norm_body(r, carry):
        rows = pl.ds(r * norm_chunk, norm_chunk)
        y = o_ref[rows]
        xc = y - channel_sum(y) * (1.0 / CONV_CH)
        var = channel_sum(xc * xc) * (1.0 / CONV_CH)
        y = xc * lax.rsqrt(var + EPS) * ln_g + ln_b
        o_ref[rows] = y * jax.nn.sigmoid(y)
        return carry

    lax.fori_loop(0, tm // norm_chunk, norm_body, 0)


def _halo_specs(tm, seq, width, col_map):
    per_tile = tm // HALO_ROWS
    last = seq // HALO_ROWS - 1
    prev = pl.BlockSpec((None, HALO_ROWS, width),
                        lambda b, i, *f: (b, jnp.maximum(i * per_tile - 1, 0), col_map(*f)))
    nxt = pl.BlockSpec((None, HALO_ROWS, width),
                       lambda b, i, *f: (b, jnp.minimum((i + 1) * per_tile, last), col_map(*f)))
    return prev, nxt


def _conv_module(glu, conv_dw_w, conv_dw_b, conv_ln_g, conv_ln_b):
    batch, seq, _ = glu.shape
    tm = _row_tile(seq)
    tile = (CONV_CH // 128, 128)
    per_tile = tm // HALO_ROWS
    last = seq // HALO_ROWS - 1
    vec = pl.BlockSpec(tile, lambda b, i: (0, 0))
    glu4 = glu.reshape((batch, seq) + tile)
    out = pl.pallas_call(
        functools.partial(_conv_kernel, row_chunk=16, norm_chunk=64),
        grid=(batch, seq // tm),
        in_specs=[
            pl.BlockSpec((None, HALO_ROWS) + tile,
                         lambda b, i: (b, jnp.maximum(i * per_tile - 1, 0), 0, 0)),
            pl.BlockSpec((None, tm) + tile, lambda b, i: (b, i, 0, 0)),
            pl.BlockSpec((None, HALO_ROWS) + tile,
                         lambda b, i: (b, jnp.minimum((i + 1) * per_tile, last), 0, 0)),
            pl.BlockSpec((CONV_WIDTH,) + tile, lambda b, i: (0, 0, 0)),
            vec, vec, vec,
        ],
        out_specs=pl.BlockSpec((None, tm) + tile, lambda b, i: (b, i, 0, 0)),
        out_shape=jax.ShapeDtypeStruct((batch, seq) + tile, F32),
        scratch_shapes=[pltpu.VMEM((tm + 2 * HALO_ROWS,) + tile, F32)],
        compiler_params=_params("parallel", "parallel"),
        name="conv_module",
    )(glu4, glu4, glu4, conv_dw_w.reshape((CONV_WIDTH,) + tile), conv_dw_b.reshape(tile),
      conv_ln_g.reshape(tile), conv_ln_b.reshape(tile))
    return out.reshape(batch, seq, CONV_CH)


def _merge_kernel(o_ref, c_ref, g_ref, wa_ref, wc_ref, m_ref):
    a = jnp.dot(o_ref[...], wa_ref[...], preferred_element_type=F32)
    cb = jnp.dot(c_ref[...].astype(BF16), wc_ref[...], preferred_element_type=F32)
    m = g_ref[:, :D_MODEL].astype(F32) * a + g_ref[:, D_MODEL:].astype(F32) * cb
    m_ref[...] = m.astype(BF16)


def _merge(o2, c2, gates, w_attn_proj, w_conv_proj, tm):
    rows = o2.shape[0]
    return pl.pallas_call(
        _merge_kernel,
        grid=(rows // tm,),
        in_specs=[
            pl.BlockSpec((tm, V_WIDTH), lambda i: (i, 0)),
            pl.BlockSpec((tm, CONV_CH), lambda i: (i, 0)),
            pl.BlockSpec((tm, N_BRANCH * D_MODEL), lambda i: (i, 0)),
            _resident((V_WIDTH, D_MODEL)),
            _resident((CONV_CH, D_MODEL)),
        ],
        out_specs=pl.BlockSpec((tm, D_MODEL), lambda i: (i, 0)),
        out_shape=jax.ShapeDtypeStruct((rows, D_MODEL), BF16),
        compiler_params=_params("parallel"),
        name="merge",
    )(o2, c2, gates, w_attn_proj, w_conv_proj)


def _out_proj_kernel(x_ref, m_ref, w_ref, y_ref):
    y_ref[...] = x_ref[...] + jnp.dot(m_ref[...], w_ref[...], preferred_element_type=F32)


def _out_proj(x2, m2, w_out, tm):
    rows = x2.shape[0]
    return pl.pallas_call(
        _out_proj_kernel,
        grid=(rows // tm,),
        in_specs=[
            pl.BlockSpec((tm, D_MODEL), lambda i: (i, 0)),
            pl.BlockSpec((tm, D_MODEL), lambda i: (i, 0)),
            _resident((D_MODEL, D_MODEL)),
        ],
        out_specs=pl.BlockSpec((tm, D_MODEL), lambda i: (i, 0)),
        out_shape=jax.ShapeDtypeStruct((rows, D_MODEL), F32),
        compiler_params=_params("parallel"),
        name="out_proj",
    )(x2, m2, w_out)


def _ffn_up_kernel(x_ref, g_ref, w_ref, up_ref, h_ref):
    @pl.when(pl.program_id(1) == 0)
    def _():
        x = x_ref[...]
        ms = jnp.mean(x * x, axis=-1, keepdims=True)
        h_ref[...] = (x * lax.rsqrt(ms + EPS) * g_ref[...]).astype(BF16)

    up_ref[...] = jnp.dot(h_ref[...], w_ref[...], preferred_element_type=F32).astype(BF16)


def _ffn_up(x2, norm2_g, w_up, tm):
    rows = x2.shape[0]
    tn = 1408
    return pl.pallas_call(
        _ffn_up_kernel,
        grid=(rows // tm, 2 * FFN_DIM // tn),
        in_specs=[
            pl.BlockSpec((tm, D_MODEL), lambda i, j: (i, 0)),
            pl.BlockSpec((1, D_MODEL), lambda i, j: (0, 0)),
            pl.BlockSpec((D_MODEL, tn), lambda i, j: (0, j)),
        ],
        out_specs=pl.BlockSpec((tm, tn), lambda i, j: (i, j)),
        out_shape=jax.ShapeDtypeStruct((rows, 2 * FFN_DIM), BF16),
        scratch_shapes=[pltpu.VMEM((tm, D_MODEL), BF16)],
        compiler_params=_params("parallel", "arbitrary"),
        name="ffn_up",
    )(x2, norm2_g, w_up)


def _conv3(prev_ref, cur_ref, next_ref, w_ref, b_ref, first, last):
    cur = cur_ref[...].astype(F32)
    tm = cur.shape[0]
    row = lax.broadcasted_iota(jnp.int32, cur.shape, 0)
    before = jnp.where(first, 0.0, prev_ref[HALO_ROWS - 1:HALO_ROWS, :].astype(F32))
    after = jnp.where(last, 0.0, next_ref[0:1, :].astype(F32))
    up_m1 = jnp.where(row == 0, before, pltpu.roll(cur, 1, 0))
    up_p1 = jnp.where(row == tm - 1, after, pltpu.roll(cur, tm - 1, 0))
    return w_ref[0:1, :] * up_m1 + w_ref[1:2, :] * cur + w_ref[2:3, :] * up_p1 + b_ref[...]


def _ffn_down_kernel(pa_ref, a_ref, na_ref, pb_ref, b_ref, nb_ref, wa_ref, ba_ref, wb_ref, bb_ref,
                     wd_ref, x_ref, y_ref):
    i = pl.program_id(1)
    f = pl.program_id(2)
    first = i == 0
    last = i == pl.num_programs(1) - 1
    fa = _conv3(pa_ref, a_ref, na_ref, wa_ref, ba_ref, first, last)
    fb = _conv3(pb_ref, b_ref, nb_ref, wb_ref, bb_ref, first, last)
    act = (fa * jax.nn.sigmoid(fa) * fb).astype(BF16)
    contrib = jnp.dot(act, wd_ref[...], preferred_element_type=F32)

    @pl.when(f == 0)
    def _():
        y_ref[...] = x_ref[...] + contrib

    @pl.when(f > 0)
    def _():
        y_ref[...] += contrib


def _ffn_down(up, x1, ffn_dw_w, ffn_dw_b, w_down):
    batch, seq, _ = up.shape
    tm = 256
    tf = 1408
    nf = FFN_DIM // tf
    prev_a, next_a = _halo_specs(tm, seq, tf, lambda f: f)
    prev_b, next_b = _halo_specs(tm, seq, tf, lambda f: f + nf)
    return pl.pallas_call(
        _ffn_down_kernel,
        grid=(batch, seq // tm, nf),
        in_specs=[
            prev_a, pl.BlockSpec((None, tm, tf), lambda b, i, f: (b, i, f)), next_a,
            prev_b, pl.BlockSpec((None, tm, tf), lambda b, i, f: (b, i, f + nf)), next_b,
            pl.BlockSpec((FFN_CONV_WIDTH, tf), lambda b, i, f: (0, f)),
            pl.BlockSpec((1, tf), lambda b, i, f: (0, f)),
            pl.BlockSpec((FFN_CONV_WIDTH, tf), lambda b, i, f: (0, f + nf)),
            pl.BlockSpec((1, tf), lambda b, i, f: (0, f + nf)),
            pl.BlockSpec((tf, D_MODEL), lambda b, i, f: (f, 0)),
            pl.BlockSpec((None, tm, D_MODEL), lambda b, i, f: (b, i, 0)),
        ],
        out_specs=pl.BlockSpec((None, tm, D_MODEL), lambda b, i, f: (b, i, 0)),
        out_shape=jax.ShapeDtypeStruct((batch, seq, D_MODEL), F32),
        compiler_params=_params("parallel", "parallel", "arbitrary"),
        name="ffn_down",
    )(up, up, up, up, up, up, ffn_dw_w, ffn_dw_b, ffn_dw_w, ffn_dw_b, w_down, x1)


def _rope_tables(seq):
    inv_freq = 1.0 / (ROPE_THETA ** (jnp.arange(0, HEAD_DIM, 2, dtype=F32) / HEAD_DIM))
    ang = jnp.arange(seq, dtype=F32)[:, None] * inv_freq[None, :]
    ang = jnp.concatenate([ang, ang], axis=-1)
    sign = jnp.where(jnp.arange(HEAD_DIM) < HEAD_DIM // 2, -1.0, 1.0).astype(F32)
    return jnp.cos(ang), jnp.sin(ang) * sign


def _encoder_layer(x, layer_idx, p):
    batch, seq, _ = x.shape
    rows = batch * seq
    tm = _row_tile(seq)
    lam_init = 0.8 - 0.6 * math.exp(-0.3 * layer_idx)
    cos, sin_signed = _rope_tables(seq)
    x2 = x.reshape(rows, D_MODEL)

    qt, k, vt, glu, gates = _in_proj(x2, seq, p["norm1_g"], p["w_in"], p["b_gate"], p["q_norm_g"],
                                     p["k_norm_g"], cos, sin_signed)
    o = _attention(qt, k.reshape(batch, seq, QK_WIDTH), vt, batch, seq, p["lambda_q1"],
                   p["lambda_k1"], p["lambda_q2"], p["lambda_k2"], p["subln_g"], lam_init)
    c = _conv_module(glu.reshape(batch, seq, CONV_CH), p["conv_dw_w"], p["conv_dw_b"],
                     p["conv_ln_g"], p["conv_ln_b"])
    m = _merge(o.reshape(rows, V_WIDTH), c.reshape(rows, CONV_CH), gates, p["w_attn_proj"],
               p["w_conv_proj"], tm)
    x1 = _out_proj(x2, m, p["w_out"], tm)
    up = _ffn_up(x1, p["norm2_g"], p["w_up"], tm)
    y = _ffn_down(up.reshape(batch, seq, 2 * FFN_DIM), x1.reshape(batch, seq, D_MODEL),
                  p["ffn_dw_w"], p["ffn_dw_b"], p["w_down"])
    return y


_MATRICES = ("w_in", "w_attn_proj", "w_conv_proj", "w_out", "w_up", "w_down")
_ROW_VECTORS = ("norm1_g", "b_gate", "q_norm_g", "k_norm_g", "lambda_q1", "lambda_k1", "lambda_q2",
                "lambda_k2", "subln_g", "conv_dw_b", "conv_ln_g", "conv_ln_b", "norm2_g", "ffn_dw_b")


def kernel(x_prompt, x_sample, norm1_g, w_in, b_gate, q_norm_g, k_norm_g, lambda_q1, lambda_k1, lambda_q2, lambda_k2, subln_g, w_attn_proj, conv_dw_w, conv_dw_b, conv_ln_g, conv_ln_b, w_conv_proj, w_out, norm2_g, w_up, ffn_dw_w, ffn_dw_b, w_down):
    stacked = dict(norm1_g=norm1_g, w_in=w_in, b_gate=b_gate, q_norm_g=q_norm_g, k_norm_g=k_norm_g,
                   lambda_q1=lambda_q1, lambda_k1=lambda_k1, lambda_q2=lambda_q2,
                   lambda_k2=lambda_k2, subln_g=subln_g, w_attn_proj=w_attn_proj,
                   conv_dw_w=conv_dw_w, conv_dw_b=conv_dw_b, conv_ln_g=conv_ln_g,
                   conv_ln_b=conv_ln_b, w_conv_proj=w_conv_proj, w_out=w_out, norm2_g=norm2_g,
                   w_up=w_up, ffn_dw_w=ffn_dw_w, ffn_dw_b=ffn_dw_b, w_down=w_down)
    depth = w_in.shape[0]
    layers = []
    for l in range(depth):
        p = {}
        for name, value in stacked.items():
            value = value[l]
            if name in _MATRICES:
                value = value.astype(BF16)
            elif name in _ROW_VECTORS:
                value = value.astype(F32)[None, :]
            else:
                value = value.astype(F32)
            p[name] = value
        layers.append(p)

    def run(x):
        for l, p in enumerate(layers):
            x = _encoder_layer(x, l, p)
        return x

    return run(x_prompt), run(x_sample)
```

```python
import functools
import math

import jax
import jax.numpy as jnp
from jax import lax
from jax.experimental import pallas as pl
from jax.experimental.pallas import tpu as pltpu

F32 = jnp.float32
BF16 = jnp.bfloat16

D_MODEL = 2048
N_HEADS = 8
HEAD_DIM = 128
V_HEAD_DIM = 2 * HEAD_DIM
QK_WIDTH = 2 * N_HEADS * HEAD_DIM
V_WIDTH = N_HEADS * V_HEAD_DIM
CONV_CH = 1024
CONV_WIDTH = 31
N_BRANCH = 2
FFN_DIM = 5632
FFN_CONV_WIDTH = 3
ROPE_THETA = 10000.0
EPS = 1e-6
IN_WIDTH = 2 * QK_WIDTH + V_WIDTH + 2 * CONV_CH + N_BRANCH * D_MODEL

VMEM_LIMIT_BYTES = 56 * 1024 * 1024
HALO_ROWS = 16
IN_COL_TILE = 2048
VT_ROWS = V_HEAD_DIM + 16
LOG2E = math.log2(math.e)


def _params(*semantics):
    return pltpu.CompilerParams(dimension_semantics=semantics, vmem_limit_bytes=VMEM_LIMIT_BYTES)


def _resident(shape):
    return pl.BlockSpec(shape, lambda *_: (0,) * len(shape), pipeline_mode=pl.Buffered(1))


def _row_tile(seq):
    return min(512, seq)


def _sigmoid(x):
    return 0.5 * jnp.tanh(0.5 * x) + 0.5


def _qk_epilogue(z, gain, cos, sin_signed, out_scale):
    outs = []
    for h in range(z.shape[1] // HEAD_DIM):
        zh = z[:, h * HEAD_DIM:(h + 1) * HEAD_DIM]
        ms = jnp.mean(zh * zh, axis=-1, keepdims=True)
        y = zh * lax.rsqrt(ms + EPS) * gain
        y = y * cos + pltpu.roll(y, HEAD_DIM // 2, 1) * sin_signed
        outs.append((y * out_scale).astype(BF16))
    return jnp.concatenate(outs, axis=1)


def _q_epilogue_t(z, gain_col, cos_t, sin_t, out_scale):
    tm = z.shape[0]
    gain = jnp.concatenate([gain_col * out_scale] * (tm // HEAD_DIM), axis=1)
    outs = []
    for h in range(z.shape[1] // (2 * HEAD_DIM)):
        zt = z[:, h * 2 * HEAD_DIM:(h + 1) * 2 * HEAD_DIM].T
        subs = []
        for sub in range(2):
            x = zt[sub * HEAD_DIM:(sub + 1) * HEAD_DIM]
            ms = jnp.mean(x * x, axis=0, keepdims=True)
            y = x * lax.rsqrt(ms + EPS) * gain
            swapped = jnp.concatenate([y[HEAD_DIM // 2:], y[:HEAD_DIM // 2]], axis=0)
            subs.append(y * cos_t + swapped * sin_t)
        outs.append(jnp.concatenate(subs, axis=0).astype(BF16))
    return outs


def _in_proj_kernel(x_ref, g1_ref, w_ref, bg_ref, qg_ref, kg_ref, cos_ref, sin_ref, cost_ref,
                    sint_ref, qt_ref, k_ref, vt_ref, glu_ref, gate_ref, h_ref, *, q_scale):
    j = pl.program_id(1)

    @pl.when(j == 0)
    def _():
        x = x_ref[...]
        ms = jnp.mean(x * x, axis=-1, keepdims=True)
        h_ref[...] = (x * lax.rsqrt(ms + EPS) * g1_ref[...]).astype(BF16)

    z = jnp.dot(h_ref[...], w_ref[...], preferred_element_type=F32)

    @pl.when(j == 0)
    def _():
        heads = _q_epilogue_t(z, qg_ref[...], cost_ref[...], sint_ref[...], q_scale)
        for h, head in enumerate(heads):
            qt_ref[h] = head

    @pl.when(j == 1)
    def _():
        k_ref[...] = _qk_epilogue(z, kg_ref[...], cos_ref[...], sin_ref[...], 1.0)

    @pl.when(j == 2)
    def _():
        tm = z.shape[0]
        for h in range(z.shape[1] // V_HEAD_DIM):
            vt_ref[h, :V_HEAD_DIM, :] = z[:, h * V_HEAD_DIM:(h + 1) * V_HEAD_DIM].T.astype(BF16)
            vt_ref[h, V_HEAD_DIM:, :] = jnp.ones((VT_ROWS - V_HEAD_DIM, tm), BF16)

    @pl.when(j == 3)
    def _():
        glu_ref[...] = z[:, :CONV_CH] * _sigmoid(z[:, CONV_CH:])

    @pl.when(j >= 4)
    def _():
        gate_ref[...] = _sigmoid(z + bg_ref[...]).astype(BF16)


def _in_proj(x2, seq, norm1_g, w_in, b_gate, q_norm_g, k_norm_g, cos, sin_signed):
    rows = x2.shape[0]
    tm = _row_tile(seq)
    tn = IN_COL_TILE
    assert tn == QK_WIDTH == V_WIDTH == 2 * CONV_CH == D_MODEL
    seq_tiles = seq // tm
    gate_step0 = (2 * QK_WIDTH + V_WIDTH + 2 * CONV_CH) // tn
    q_gain_col = jnp.broadcast_to(q_norm_g.reshape(HEAD_DIM, 1), (HEAD_DIM, HEAD_DIM))

    def gate_col(j):
        return jnp.clip(j - gate_step0, 0, N_BRANCH - 1)

    kernel = functools.partial(_in_proj_kernel, q_scale=HEAD_DIM ** -0.5 * LOG2E)
    out_shapes = (
        jax.ShapeDtypeStruct((rows // tm, N_HEADS, 2 * HEAD_DIM, tm), BF16),
        jax.ShapeDtypeStruct((rows, QK_WIDTH), BF16),
        jax.ShapeDtypeStruct((rows // tm, N_HEADS, VT_ROWS, tm), BF16),
        jax.ShapeDtypeStruct((rows, CONV_CH), F32),
        jax.ShapeDtypeStruct((rows, N_BRANCH * D_MODEL), BF16),
    )
    return pl.pallas_call(
        kernel,
        grid=(rows // tm, IN_WIDTH // tn),
        in_specs=[
            pl.BlockSpec((tm, D_MODEL), lambda i, j: (i, 0)),
            pl.BlockSpec((1, D_MODEL), lambda i, j: (0, 0)),
            pl.BlockSpec((D_MODEL, tn), lambda i, j: (0, j)),
            pl.BlockSpec((1, tn), lambda i, j: (0, gate_col(j))),
            pl.BlockSpec((HEAD_DIM, HEAD_DIM), lambda i, j: (0, 0)),
            pl.BlockSpec((1, HEAD_DIM), lambda i, j: (0, 0)),
            pl.BlockSpec((tm, HEAD_DIM), lambda i, j: (i % seq_tiles, 0)),
            pl.BlockSpec((tm, HEAD_DIM), lambda i, j: (i % seq_tiles, 0)),
            pl.BlockSpec((HEAD_DIM, tm), lambda i, j: (0, i % seq_tiles)),
            pl.BlockSpec((HEAD_DIM, tm), lambda i, j: (0, i % seq_tiles)),
        ],
        out_specs=(
            pl.BlockSpec((None, N_HEADS, 2 * HEAD_DIM, tm), lambda i, j: (i, 0, 0, 0)),
            pl.BlockSpec((tm, tn), lambda i, j: (i, 0)),
            pl.BlockSpec((None, N_HEADS, VT_ROWS, tm), lambda i, j: (i, 0, 0, 0)),
            pl.BlockSpec((tm, CONV_CH), lambda i, j: (i, 0)),
            pl.BlockSpec((tm, tn), lambda i, j: (i, gate_col(j))),
        ),
        out_shape=out_shapes,
        scratch_shapes=[pltpu.VMEM((tm, D_MODEL), BF16)],
        compiler_params=_params("parallel", "arbitrary"),
        name="in_proj",
    )(x2, norm1_g, w_in, b_gate, q_gain_col, k_norm_g, cos, sin_signed, cos.T, sin_signed.T)


def _attn_kernel(lq1_ref, lk1_ref, lq2_ref, lk2_ref, sg_ref, qt_ref, k_ref, vt_ref, o_ref,
                 s0_ref, s1_ref, x0_ref, x1_ref, p0_ref, p1_ref, a0_ref, a1_ref, m_ref, acc_ref,
                 *, tq, k_chunk, lam_init):
    n_chunks = k_ref.shape[0] // k_chunk
    n_units = qt_ref.shape[0] * n_chunks
    s_bufs, p_bufs = (s0_ref, s1_ref), (p0_ref, p1_ref)
    a_bufs, max_bufs = (a0_ref, a1_ref), (x0_ref, x1_ref)

    def scores(u, slot):
        c0 = pl.multiple_of((u % n_chunks) * k_chunk, k_chunk)
        for sub in range(2):
            dims = slice(sub * HEAD_DIM, (sub + 1) * HEAD_DIM)
            st = jnp.dot(k_ref[pl.ds(c0, k_chunk), dims], qt_ref[u // n_chunks, dims, :],
                         preferred_element_type=F32)
            s_bufs[slot][sub] = st
            max_bufs[slot][sub] = jnp.broadcast_to(jnp.max(st, axis=0, keepdims=True), (8, tq))

    def softmax(u, slot):
        m_old = m_ref[...]
        if slot == 0:
            m_old = m_old + jnp.where(u % n_chunks == 0, -jnp.inf, 0.0)
        m_new = jnp.maximum(m_old, max_bufs[slot][...])
        a_bufs[slot][...] = jnp.exp2(m_old - m_new)
        m_ref[...] = m_new
        for sub in range(2):
            d = s_bufs[slot][sub] - m_new[sub, 0:1, :]
            p_bufs[slot][sub] = jnp.exp2(d).astype(BF16)

    def values(u, slot):
        vt = vt_ref[u % n_chunks]
        for sub in range(2):
            pv = jnp.dot(vt, p_bufs[slot][sub], preferred_element_type=F32)
            acc_ref[sub] = a_bufs[slot][sub, 0:1, :] * acc_ref[sub] + pv

    def normalise(tile):
        lam = (jnp.exp(jnp.sum(lq1_ref[...] * lk1_ref[...], axis=-1, keepdims=True))
               - jnp.exp(jnp.sum(lq2_ref[...] * lk2_ref[...], axis=-1, keepdims=True)) + lam_init)
        o0 = acc_ref[0, :V_HEAD_DIM, :] / acc_ref[0, V_HEAD_DIM:V_HEAD_DIM + 1, :]
        o1 = acc_ref[1, :V_HEAD_DIM, :] / acc_ref[1, V_HEAD_DIM:V_HEAD_DIM + 1, :]
        o = o0 - lam * o1
        ms = jnp.mean(o * o, axis=0, keepdims=True)
        gain = jnp.concatenate([sg_ref[...]] * (tq // HEAD_DIM), axis=1)
        o = o * lax.rsqrt(ms + EPS) * gain * (1.0 - lam_init)
        o_ref[pl.ds(pl.multiple_of(tile * tq, tq), tq), :] = o.T.astype(BF16)

    m_ref[...] = jnp.full(m_ref.shape, -jnp.inf, F32)
    acc_ref[...] = jnp.zeros(acc_ref.shape, F32)

    scores(0, 0)
    scores(1, 1)
    softmax(0, 0)

    def body(t, carry):
        u = 2 * t + 2
        scores(u, 0)
        softmax(u - 1, 1)
        values(u - 2, 0)
        scores(u + 1, 1)
        softmax(u, 0)
        values(u - 1, 1)

        @pl.when(u % n_chunks == 0)
        def _():
            normalise(u // n_chunks - 1)

        return carry

    lax.fori_loop(0, n_units // 2 - 1, body, 0)
    softmax(n_units - 1, 1)
    values(n_units - 2, 0)
    values(n_units - 1, 1)
    normalise(n_units // n_chunks - 1)


def _attention(qt, k, vt, batch, seq, lq1, lk1, lq2, lk2, subln_g, lam_init):
    tq = qt.shape[-1]
    k_chunk = vt.shape[-1]
    tq_group = min(2048, seq)
    groups = seq // tq_group
    assert (seq // k_chunk) % 2 == 0
    gain = jnp.broadcast_to(subln_g.reshape(V_HEAD_DIM, 1), (V_HEAD_DIM, HEAD_DIM))
    vec = pl.BlockSpec((1, HEAD_DIM), lambda b, h, i: (0, 0))
    kernel = functools.partial(_attn_kernel, tq=tq, k_chunk=k_chunk, lam_init=lam_init)
    s_buf = pltpu.VMEM((2, k_chunk, tq), F32)
    p_buf = pltpu.VMEM((2, k_chunk, tq), BF16)
    stat = pltpu.VMEM((2, 8, tq), F32)
    return pl.pallas_call(
        kernel,
        grid=(batch, N_HEADS, groups),
        in_specs=[
            vec, vec, vec, vec,
            pl.BlockSpec((V_HEAD_DIM, HEAD_DIM), lambda b, h, i: (0, 0)),
            pl.BlockSpec((tq_group // tq, None, 2 * HEAD_DIM, tq),
                         lambda b, h, i: (b * groups + i, h, 0, 0)),
            pl.BlockSpec((None, seq, 2 * HEAD_DIM), lambda b, h, i: (b, 0, h)),
            pl.BlockSpec((seq // k_chunk, None, VT_ROWS, k_chunk), lambda b, h, i: (b, h, 0, 0)),
        ],
        out_specs=pl.BlockSpec((None, tq_group, V_HEAD_DIM), lambda b, h, i: (b, i, h)),
        out_shape=jax.ShapeDtypeStruct((batch, seq, V_WIDTH), BF16),
        scratch_shapes=[s_buf, s_buf, stat, stat, p_buf, p_buf, stat, stat, stat,
                        pltpu.VMEM((2, VT_ROWS, tq), F32)],
        compiler_params=_params("parallel", "parallel", "arbitrary"),
        name="diff_attention",
    )(lq1, lk1, lq2, lk2, gain, qt, k, vt)


def _conv_kernel(prev_ref, cur_ref, next_ref, w_ref, b_ref, lg_ref, lb_ref, o_ref, ext_ref,
                 *, row_chunk, norm_chunk):
    i = pl.program_id(1)
    tm = cur_ref.shape[0]
    pad = (CONV_WIDTH - 1) // 2
    ext_ref[0:HALO_ROWS] = jnp.where(i > 0, prev_ref[...], 0.0)
    ext_ref[HALO_ROWS:HALO_ROWS + tm] = cur_ref[...]
    ext_ref[HALO_ROWS + tm:] = jnp.where(i < pl.num_programs(1) - 1, next_ref[...], 0.0)
    bias, ln_g, ln_b = b_ref[...], lg_ref[...], lb_ref[...]

    def channel_sum(x):
        return jnp.sum(jnp.sum(x, axis=2, keepdims=True), axis=1, keepdims=True)

    def conv_body(r, carry):
        r0 = r * row_chunk
        acc = jnp.broadcast_to(bias, (row_chunk,) + bias.shape)
        for t in range(CONV_WIDTH):
            off = HALO_ROWS - pad + t
            acc = acc + w_ref[t] * ext_ref[pl.ds(r0 + off, row_chunk)]
        o_ref[pl.ds(r0, row_chunk)] = acc
        return carry

    lax.fori_loop(0, tm // row_chunk, conv_body, 0)

    def norm_body(r, carry):
        rows = pl.ds(r * norm_chunk, norm_chunk)
        y = o_ref[rows]
        xc = y - channel_sum(y) * (1.0 / CONV_CH)
        var = channel_sum(xc * xc) * (1.0 / CONV_CH)
        y = xc * lax.rsqrt(var + EPS) * ln_g + ln_b
        o_ref[rows] = y * jax.nn.sigmoid(y)
        return carry

    lax.fori_loop(0, tm // norm_chunk, norm_body, 0)


def _halo_specs(tm, seq, width, col_map):
    per_tile = tm // HALO_ROWS
    last = seq // HALO_ROWS - 1
    prev = pl.BlockSpec((None, HALO_ROWS, width),
                        lambda b, i, *f: (b, jnp.maximum(i * per_tile - 1, 0), col_map(*f)))
    nxt = pl.BlockSpec((None, HALO_ROWS, width),
                       lambda b, i, *f: (b, jnp.minimum((i + 1) * per_tile, last), col_map(*f)))
    return prev, nxt


def _conv_module(glu, conv_dw_w, conv_dw_b, conv_ln_g, conv_ln_b):
    batch, seq, _ = glu.shape
    tm = _row_tile(seq)
    tile = (CONV_CH // 128, 128)
    per_tile = tm // HALO_ROWS
    last = seq // HALO_ROWS - 1
    vec = pl.BlockSpec(tile, lambda b, i: (0, 0))
    glu4 = glu.reshape((batch, seq) + tile)
    out = pl.pallas_call(
        functools.partial(_conv_kernel, row_chunk=16, norm_chunk=64),
        grid=(batch, seq // tm),
        in_specs=[
            pl.BlockSpec((None, HALO_ROWS) + tile,
                         lambda b, i: (b, jnp.maximum(i * per_tile - 1, 0), 0, 0)),
            pl.BlockSpec((None, tm) + tile, lambda b, i: (b, i, 0, 0)),
            pl.BlockSpec((None, HALO_ROWS) + tile,
                         lambda b, i: (b, jnp.minimum((i + 1) * per_tile, last), 0, 0)),
            pl.BlockSpec((CONV_WIDTH,) + tile, lambda b, i: (0, 0, 0)),
            vec, vec, vec,
        ],
        out_specs=pl.BlockSpec((None, tm) + tile, lambda b, i: (b, i, 0, 0)),
        out_shape=jax.ShapeDtypeStruct((batch, seq) + tile, F32),
        scratch_shapes=[pltpu.VMEM((tm + 2 * HALO_ROWS,) + tile, F32)],
        compiler_params=_params("parallel", "parallel"),
        name="conv_module",
    )(glu4, glu4, glu4, conv_dw_w.reshape((CONV_WIDTH,) + tile), conv_dw_b.reshape(tile),
      conv_ln_g.reshape(tile), conv_ln_b.reshape(tile))
    return out.reshape(batch, seq, CONV_CH)


def _merge_kernel(o_ref, c_ref, g_ref, wa_ref, wc_ref, m_ref):
    a = jnp.dot(o_ref[...], wa_ref[...], preferred_element_type=F32)
    cb = jnp.dot(c_ref[...].astype(BF16), wc_ref[...], preferred_element_type=F32)
    m = g_ref[:, :D_MODEL].astype(F32) * a + g_ref[:, D_MODEL:].astype(F32) * cb
    m_ref[...] = m.astype(BF16)


def _merge(o2, c2, gates, w_attn_proj, w_conv_proj, tm):
    rows = o2.shape[0]
    return pl.pallas_call(
        _merge_kernel,
        grid=(rows // tm,),
        in_specs=[
            pl.BlockSpec((tm, V_WIDTH), lambda i: (i, 0)),
            pl.BlockSpec((tm, CONV_CH), lambda i: (i, 0)),
            pl.BlockSpec((tm, N_BRANCH * D_MODEL), lambda i: (i, 0)),
            _resident((V_WIDTH, D_MODEL)),
            _resident((CONV_CH, D_MODEL)),
        ],
        out_specs=pl.BlockSpec((tm, D_MODEL), lambda i: (i, 0)),
        out_shape=jax.ShapeDtypeStruct((rows, D_MODEL), BF16),
        compiler_params=_params("parallel"),
        name="merge",
    )(o2, c2, gates, w_attn_proj, w_conv_proj)


def _out_proj_kernel(x_ref, m_ref, w_ref, y_ref):
    y_ref[...] = x_ref[...] + jnp.dot(m_ref[...], w_ref[...], preferred_element_type=F32)


def _out_proj(x2, m2, w_out, tm):
    rows = x2.shape[0]
    return pl.pallas_call(
        _out_proj_kernel,
        grid=(rows // tm,),
        in_specs=[
            pl.BlockSpec((tm, D_MODEL), lambda i: (i, 0)),
            pl.BlockSpec((tm, D_MODEL), lambda i: (i, 0)),
            _resident((D_MODEL, D_MODEL)),
        ],
        out_specs=pl.BlockSpec((tm, D_MODEL), lambda i: (i, 0)),
        out_shape=jax.ShapeDtypeStruct((rows, D_MODEL), F32),
        compiler_params=_params("parallel"),
        name="out_proj",
    )(x2, m2, w_out)


def _ffn_up_kernel(x_ref, g_ref, w_ref, up_ref, h_ref):
    @pl.when(pl.program_id(1) == 0)
    def _():
        x = x_ref[...]
        ms = jnp.mean(x * x, axis=-1, keepdims=True)
        h_ref[...] = (x * lax.rsqrt(ms + EPS) * g_ref[...]).astype(BF16)

    up_ref[...] = jnp.dot(h_ref[...], w_ref[...], preferred_element_type=F32).astype(BF16)


def _ffn_up(x2, norm2_g, w_up, tm):
    rows = x2.shape[0]
    tn = 1408
    return pl.pallas_call(
        _ffn_up_kernel,
        grid=(rows // tm, 2 * FFN_DIM // tn),
        in_specs=[
            pl.BlockSpec((tm, D_MODEL), lambda i, j: (i, 0)),
            pl.BlockSpec((1, D_MODEL), lambda i, j: (0, 0)),
            pl.BlockSpec((D_MODEL, tn), lambda i, j: (0, j)),
        ],
        out_specs=pl.BlockSpec((tm, tn), lambda i, j: (i, j)),
        out_shape=jax.ShapeDtypeStruct((rows, 2 * FFN_DIM), BF16),
        scratch_shapes=[pltpu.VMEM((tm, D_MODEL), BF16)],
        compiler_params=_params("parallel", "arbitrary"),
        name="ffn_up",
    )(x2, norm2_g, w_up)


def _conv3(prev_ref, cur_ref, next_ref, w_ref, b_ref, first, last):
    cur = cur_ref[...].astype(F32)
    tm = cur.shape[0]
    row = lax.broadcasted_iota(jnp.int32, cur.shape, 0)
    before = jnp.where(first, 0.0, prev_ref[HALO_ROWS - 1:HALO_ROWS, :].astype(F32))
    after = jnp.where(last, 0.0, next_ref[0:1, :].astype(F32))
    up_m1 = jnp.where(row == 0, before, pltpu.roll(cur, 1, 0))
    up_p1 = jnp.where(row == tm - 1, after, pltpu.roll(cur, tm - 1, 0))
    return w_ref[0:1, :] * up_m1 + w_ref[1:2, :] * cur + w_ref[2:3, :] * up_p1 + b_ref[...]


def _ffn_down_kernel(pa_ref, a_ref, na_ref, pb_ref, b_ref, nb_ref, wa_ref, ba_ref, wb_ref, bb_ref,
                     wd_ref, x_ref, y_ref):
    i = pl.program_id(1)
    first = i == 0
    last = i == pl.num_programs(1) - 1
    fa = _conv3(pa_ref, a_ref, na_ref, wa_ref, ba_ref, first, last)
    fb = _conv3(pb_ref, b_ref, nb_ref, wb_ref, bb_ref, first, last)
    act = (fa * jax.nn.sigmoid(fa) * fb).astype(BF16)
    y_ref[...] = x_ref[...] + jnp.dot(act, wd_ref[...], preferred_element_type=F32)


def _ffn_down(up, x1, ffn_dw_w, ffn_dw_b, w_down):
    batch, seq, _ = up.shape
    tm = min(256, seq)
    prev_a, next_a = _halo_specs(tm, seq, FFN_DIM, lambda: 0)
    prev_b, next_b = _halo_specs(tm, seq, FFN_DIM, lambda: 1)
    return pl.pallas_call(
        _ffn_down_kernel,
        grid=(batch, seq // tm),
        in_specs=[
            prev_a, pl.BlockSpec((None, tm, FFN_DIM), lambda b, i: (b, i, 0)), next_a,
            prev_b, pl.BlockSpec((None, tm, FFN_DIM), lambda b, i: (b, i, 1)), next_b,
            pl.BlockSpec((FFN_CONV_WIDTH, FFN_DIM), lambda b, i: (0, 0)),
            pl.BlockSpec((1, FFN_DIM), lambda b, i: (0, 0)),
            pl.BlockSpec((FFN_CONV_WIDTH, FFN_DIM), lambda b, i: (0, 1)),
            pl.BlockSpec((1, FFN_DIM), lambda b, i: (0, 1)),
            _resident((FFN_DIM, D_MODEL)),
            pl.BlockSpec((None, tm, D_MODEL), lambda b, i: (b, i, 0)),
        ],
        out_specs=pl.BlockSpec((None, tm, D_MODEL), lambda b, i: (b, i, 0)),
        out_shape=jax.ShapeDtypeStruct((batch, seq, D_MODEL), F32),
        compiler_params=_params("parallel", "parallel"),
        name="ffn_down",
    )(up, up, up, up, up, up, ffn_dw_w, ffn_dw_b, ffn_dw_w, ffn_dw_b, w_down, x1)


def _rope_tables(seq):
    inv_freq = 1.0 / (ROPE_THETA ** (jnp.arange(0, HEAD_DIM, 2, dtype=F32) / HEAD_DIM))
    ang = jnp.arange(seq, dtype=F32)[:, None] * inv_freq[None, :]
    ang = jnp.concatenate([ang, ang], axis=-1)
    sign = jnp.where(jnp.arange(HEAD_DIM) < HEAD_DIM // 2, -1.0, 1.0).astype(F32)
    return jnp.cos(ang), jnp.sin(ang) * sign


def _encoder_layer(x, layer_idx, p):
    batch, seq, _ = x.shape
    rows = batch * seq
    tm = _row_tile(seq)
    lam_init = 0.8 - 0.6 * math.exp(-0.3 * layer_idx)
    cos, sin_signed = _rope_tables(seq)
    x2 = x.reshape(rows, D_MODEL)

    qt, k, vt, glu, gates = _in_proj(x2, seq, p["norm1_g"], p["w_in"], p["b_gate"], p["q_norm_g"],
                                     p["k_norm_g"], cos, sin_signed)
    o = _attention(qt, k.reshape(batch, seq, QK_WIDTH), vt, batch, seq, p["lambda_q1"],
                   p["lambda_k1"], p["lambda_q2"], p["lambda_k2"], p["subln_g"], lam_init)
    c = _conv_module(glu.reshape(batch, seq, CONV_CH), p["conv_dw_w"], p["conv_dw_b"],
                     p["conv_ln_g"], p["conv_ln_b"])
    m = _merge(o.reshape(rows, V_WIDTH), c.reshape(rows, CONV_CH), gates, p["w_attn_proj"],
               p["w_conv_proj"], tm)
    x1 = _out_proj(x2, m, p["w_out"], tm)
    up = _ffn_up(x1, p["norm2_g"], p["w_up"], tm)
    y = _ffn_down(up.reshape(batch, seq, 2 * FFN_DIM), x1.reshape(batch, seq, D_MODEL),
                  p["ffn_dw_w"], p["ffn_dw_b"], p["w_down"])
    return y


_MATRICES = ("w_in", "w_attn_proj", "w_conv_proj", "w_out", "w_up", "w_down")
_ROW_VECTORS = ("norm1_g", "b_gate", "q_norm_g", "k_norm_g", "lambda_q1", "lambda_k1", "lambda_q2",
                "lambda_k2", "subln_g", "conv_dw_b", "conv_ln_g", "conv_ln_b", "norm2_g", "ffn_dw_b")


def kernel(x_prompt, x_sample, norm1_g, w_in, b_gate, q_norm_g, k_norm_g, lambda_q1, lambda_k1, lambda_q2, lambda_k2, subln_g, w_attn_proj, conv_dw_w, conv_dw_b, conv_ln_g, conv_ln_b, w_conv_proj, w_out, norm2_g, w_up, ffn_dw_w, ffn_dw_b, w_down):
    stacked = dict(norm1_g=norm1_g, w_in=w_in, b_gate=b_gate, q_norm_g=q_norm_g, k_norm_g=k_norm_g,
                   lambda_q1=lambda_q1, lambda_k1=lambda_k1, lambda_q2=lambda_q2,
                   lambda_k2=lambda_k2, subln_g=subln_g, w_attn_proj=w_attn_proj,
                   conv_dw_w=conv_dw_w, conv_dw_b=conv_dw_b, conv_ln_g=conv_ln_g,
                   conv_ln_b=conv_ln_b, w_conv_proj=w_conv_proj, w_out=w_out, norm2_g=norm2_g,
                   w_up=w_up, ffn_dw_w=ffn_dw_w, ffn_dw_b=ffn_dw_b, w_down=w_down)
    depth = w_in.shape[0]
    layers = []
    for l in range(depth):
        p = {}
        for name, value in stacked.items():
            value = value[l]
            if name in _MATRICES:
                value = value.astype(BF16)
            elif name in _ROW_VECTORS:
                value = value.astype(F32)[None, :]
            else:
                value = value.astype(F32)
            p[name] = value
        layers.append(p)

    def run(x):
        for l, p in enumerate(layers):
            x = _encoder_layer(x, l, p)
        return x

    return run(x_prompt), run(x_sample)
```

```python
import functools
import math

import jax
import jax.numpy as jnp
from jax import lax
from jax.experimental import pallas as pl
from jax.experimental.pallas import tpu as pltpu

F32 = jnp.float32
BF16 = jnp.bfloat16

D_MODEL = 2048
N_HEADS = 8
HEAD_DIM = 128
V_HEAD_DIM = 2 * HEAD_DIM
QK_WIDTH = 2 * N_HEADS * HEAD_DIM
V_WIDTH = N_HEADS * V_HEAD_DIM
CONV_CH = 1024
CONV_WIDTH = 31
N_BRANCH = 2
FFN_DIM = 5632
FFN_CONV_WIDTH = 3
ROPE_THETA = 10000.0
EPS = 1e-6
IN_WIDTH = 2 * QK_WIDTH + V_WIDTH + 2 * CONV_CH + N_BRANCH * D_MODEL

VMEM_LIMIT_BYTES = 56 * 1024 * 1024
HALO_ROWS = 16
ROW_TILE = 512
IN_COL_TILE = 2048
FFN_UP_COL_TILE = 2816
FFN_DOWN_ROW_TILE = 256
ATTN_QUERY_GROUP = 4096
CONV_ROW_CHUNK = 16
NORM_ROW_CHUNK = 64
VT_ROWS = V_HEAD_DIM + 16
LOG2E = math.log2(math.e)


def _params(*semantics):
    return pltpu.CompilerParams(dimension_semantics=semantics, vmem_limit_bytes=VMEM_LIMIT_BYTES)


def _resident(shape):
    return pl.BlockSpec(shape, lambda *_: (0,) * len(shape), pipeline_mode=pl.Buffered(1))


def _row_tile(seq):
    return min(ROW_TILE, seq)


def _sigmoid(x):
    return 0.5 * jnp.tanh(0.5 * x) + 0.5


def _qk_epilogue(z, gain, cos, sin_signed, out_scale):
    outs = []
    for h in range(z.shape[1] // HEAD_DIM):
        zh = z[:, h * HEAD_DIM:(h + 1) * HEAD_DIM]
        ms = jnp.mean(zh * zh, axis=-1, keepdims=True)
        y = zh * lax.rsqrt(ms + EPS) * gain
        y = y * cos + pltpu.roll(y, HEAD_DIM // 2, 1) * sin_signed
        outs.append((y * out_scale).astype(BF16))
    return jnp.concatenate(outs, axis=1)


def _q_epilogue_t(z, gain_col, cos_t, sin_t, out_scale):
    tm = z.shape[0]
    gain = jnp.concatenate([gain_col * out_scale] * (tm // HEAD_DIM), axis=1)
    outs = []
    for h in range(z.shape[1] // (2 * HEAD_DIM)):
        zt = z[:, h * 2 * HEAD_DIM:(h + 1) * 2 * HEAD_DIM].T
        subs = []
        for sub in range(2):
            x = zt[sub * HEAD_DIM:(sub + 1) * HEAD_DIM]
            ms = jnp.mean(x * x, axis=0, keepdims=True)
            y = x * lax.rsqrt(ms + EPS) * gain
            swapped = jnp.concatenate([y[HEAD_DIM // 2:], y[:HEAD_DIM // 2]], axis=0)
            subs.append(y * cos_t + swapped * sin_t)
        outs.append(jnp.concatenate(subs, axis=0).astype(BF16))
    return outs


def _in_proj_kernel(x_ref, g1_ref, w_ref, bg_ref, qg_ref, kg_ref, cos_ref, sin_ref, cost_ref,
                    sint_ref, qt_ref, k_ref, vt_ref, glu_ref, gate_ref, h_ref, *, q_scale):
    j = pl.program_id(1)

    @pl.when(j == 0)
    def _():
        x = x_ref[...]
        ms = jnp.mean(x * x, axis=-1, keepdims=True)
        h_ref[...] = (x * lax.rsqrt(ms + EPS) * g1_ref[...]).astype(BF16)

    z = jnp.dot(h_ref[...], w_ref[...], preferred_element_type=F32)

    @pl.when(j == 0)
    def _():
        heads = _q_epilogue_t(z, qg_ref[...], cost_ref[...], sint_ref[...], q_scale)
        for h, head in enumerate(heads):
            qt_ref[h] = head

    @pl.when(j == 1)
    def _():
        k_ref[...] = _qk_epilogue(z, kg_ref[...], cos_ref[...], sin_ref[...], 1.0)

    @pl.when(j == 2)
    def _():
        tm = z.shape[0]
        for h in range(z.shape[1] // V_HEAD_DIM):
            vt_ref[h, :V_HEAD_DIM, :] = z[:, h * V_HEAD_DIM:(h + 1) * V_HEAD_DIM].T.astype(BF16)
            vt_ref[h, V_HEAD_DIM:, :] = jnp.ones((VT_ROWS - V_HEAD_DIM, tm), BF16)

    @pl.when(j == 3)
    def _():
        glu_ref[...] = z[:, :CONV_CH] * _sigmoid(z[:, CONV_CH:])

    @pl.when(j >= 4)
    def _():
        gate_ref[...] = _sigmoid(z + bg_ref[...]).astype(BF16)


def _in_proj(x2, seq, norm1_g, w_in, b_gate, q_norm_g, k_norm_g, cos, sin_signed):
    rows = x2.shape[0]
    tm = _row_tile(seq)
    tn = IN_COL_TILE
    assert tn == QK_WIDTH == V_WIDTH == 2 * CONV_CH == D_MODEL
    seq_tiles = seq // tm
    gate_step0 = (2 * QK_WIDTH + V_WIDTH + 2 * CONV_CH) // tn
    q_gain_col = jnp.broadcast_to(q_norm_g.reshape(HEAD_DIM, 1), (HEAD_DIM, HEAD_DIM))

    def gate_col(j):
        return jnp.clip(j - gate_step0, 0, N_BRANCH - 1)

    kernel = functools.partial(_in_proj_kernel, q_scale=HEAD_DIM ** -0.5 * LOG2E)
    out_shapes = (
        jax.ShapeDtypeStruct((rows // tm, N_HEADS, 2 * HEAD_DIM, tm), BF16),
        jax.ShapeDtypeStruct((rows, QK_WIDTH), BF16),
        jax.ShapeDtypeStruct((rows // tm, N_HEADS, VT_ROWS, tm), BF16),
        jax.ShapeDtypeStruct((rows, CONV_CH), F32),
        jax.ShapeDtypeStruct((rows, N_BRANCH * D_MODEL), BF16),
    )
    return pl.pallas_call(
        kernel,
        grid=(rows // tm, IN_WIDTH // tn),
        in_specs=[
            pl.BlockSpec((tm, D_MODEL), lambda i, j: (i, 0)),
            pl.BlockSpec((1, D_MODEL), lambda i, j: (0, 0)),
            pl.BlockSpec((D_MODEL, tn), lambda i, j: (0, j)),
            pl.BlockSpec((1, tn), lambda i, j: (0, gate_col(j))),
            pl.BlockSpec((HEAD_DIM, HEAD_DIM), lambda i, j: (0, 0)),
            pl.BlockSpec((1, HEAD_DIM), lambda i, j: (0, 0)),
            pl.BlockSpec((tm, HEAD_DIM), lambda i, j: (i % seq_tiles, 0)),
            pl.BlockSpec((tm, HEAD_DIM), lambda i, j: (i % seq_tiles, 0)),
            pl.BlockSpec((HEAD_DIM, tm), lambda i, j: (0, i % seq_tiles)),
            pl.BlockSpec((HEAD_DIM, tm), lambda i, j: (0, i % seq_tiles)),
        ],
        out_specs=(
            pl.BlockSpec((None, N_HEADS, 2 * HEAD_DIM, tm), lambda i, j: (i, 0, 0, 0)),
            pl.BlockSpec((tm, tn), lambda i, j: (i, 0)),
            pl.BlockSpec((None, N_HEADS, VT_ROWS, tm), lambda i, j: (i, 0, 0, 0)),
            pl.BlockSpec((tm, CONV_CH), lambda i, j: (i, 0)),
            pl.BlockSpec((tm, tn), lambda i, j: (i, gate_col(j))),
        ),
        out_shape=out_shapes,
        scratch_shapes=[pltpu.VMEM((tm, D_MODEL), BF16)],
        compiler_params=_params("parallel", "arbitrary"),
        name="in_proj",
    )(x2, norm1_g, w_in, b_gate, q_gain_col, k_norm_g, cos, sin_signed, cos.T, sin_signed.T)


def _attn_kernel(lq1_ref, lk1_ref, lq2_ref, lk2_ref, sg_ref, qt_ref, k_ref, vt_ref, o_ref,
                 s0_ref, s1_ref, x0_ref, x1_ref, p0_ref, p1_ref, a0_ref, a1_ref, m_ref, acc_ref,
                 *, tq, k_chunk, lam_init):
    n_chunks = k_ref.shape[0] // k_chunk
    n_units = qt_ref.shape[0] * n_chunks
    s_bufs, p_bufs = (s0_ref, s1_ref), (p0_ref, p1_ref)
    a_bufs, max_bufs = (a0_ref, a1_ref), (x0_ref, x1_ref)

    def scores(u, slot):
        c0 = pl.multiple_of((u % n_chunks) * k_chunk, k_chunk)
        for sub in range(2):
            dims = slice(sub * HEAD_DIM, (sub + 1) * HEAD_DIM)
            st = jnp.dot(k_ref[pl.ds(c0, k_chunk), dims], qt_ref[u // n_chunks, dims, :],
                         preferred_element_type=F32)
            s_bufs[slot][sub] = st
            max_bufs[slot][sub] = jnp.broadcast_to(jnp.max(st, axis=0, keepdims=True), (8, tq))

    def softmax(u, slot):
        m_old = m_ref[...]
        if slot == 0:
            m_old = m_old + jnp.where(u % n_chunks == 0, -jnp.inf, 0.0)
        m_new = jnp.maximum(m_old, max_bufs[slot][...])
        a_bufs[slot][...] = jnp.exp2(m_old - m_new)
        m_ref[...] = m_new
        for sub in range(2):
            d = s_bufs[slot][sub] - m_new[sub, 0:1, :]
            p_bufs[slot][sub] = jnp.exp2(d).astype(BF16)

    def values(u, slot):
        vt = vt_ref[u % n_chunks]
        for sub in range(2):
            pv = jnp.dot(vt, p_bufs[slot][sub], preferred_element_type=F32)
            acc_ref[sub] = a_bufs[slot][sub, 0:1, :] * acc_ref[sub] + pv

    def normalise(tile):
        lam = (jnp.exp(jnp.sum(lq1_ref[...] * lk1_ref[...], axis=-1, keepdims=True))
               - jnp.exp(jnp.sum(lq2_ref[...] * lk2_ref[...], axis=-1, keepdims=True)) + lam_init)
        o0 = acc_ref[0, :V_HEAD_DIM, :] / acc_ref[0, V_HEAD_DIM:V_HEAD_DIM + 1, :]
        o1 = acc_ref[1, :V_HEAD_DIM, :] / acc_ref[1, V_HEAD_DIM:V_HEAD_DIM + 1, :]
        o = o0 - lam * o1
        ms = jnp.mean(o * o, axis=0, keepdims=True)
        gain = jnp.concatenate([sg_ref[...]] * (tq // HEAD_DIM), axis=1)
        o = o * lax.rsqrt(ms + EPS) * gain * (1.0 - lam_init)
        o_ref[pl.ds(pl.multiple_of(tile * tq, tq), tq), :] = o.T.astype(BF16)

    m_ref[...] = jnp.full(m_ref.shape, -jnp.inf, F32)
    acc_ref[...] = jnp.zeros(acc_ref.shape, F32)

    scores(0, 0)
    scores(1, 1)
    softmax(0, 0)

    def body(t, carry):
        u = 2 * t + 2
        scores(u, 0)
        softmax(u - 1, 1)
        values(u - 2, 0)
        scores(u + 1, 1)
        softmax(u, 0)
        values(u - 1, 1)

        @pl.when(u % n_chunks == 0)
        def _():
            normalise(u // n_chunks - 1)

        return carry

    lax.fori_loop(0, n_units // 2 - 1, body, 0)
    softmax(n_units - 1, 1)
    values(n_units - 2, 0)
    values(n_units - 1, 1)
    normalise(n_units // n_chunks - 1)


def _attention(qt, k, vt, batch, seq, lq1, lk1, lq2, lk2, subln_g, lam_init):
    tq = qt.shape[-1]
    k_chunk = vt.shape[-1]
    tq_group = min(ATTN_QUERY_GROUP, seq)
    groups = seq // tq_group
    assert (seq // k_chunk) % 2 == 0
    gain = jnp.broadcast_to(subln_g.reshape(V_HEAD_DIM, 1), (V_HEAD_DIM, HEAD_DIM))
    vec = pl.BlockSpec((1, HEAD_DIM), lambda b, h, i: (0, 0))
    kernel = functools.partial(_attn_kernel, tq=tq, k_chunk=k_chunk, lam_init=lam_init)
    s_buf = pltpu.VMEM((2, k_chunk, tq), F32)
    p_buf = pltpu.VMEM((2, k_chunk, tq), BF16)
    stat = pltpu.VMEM((2, 8, tq), F32)
    return pl.pallas_call(
        kernel,
        grid=(batch, N_HEADS, groups),
        in_specs=[
            vec, vec, vec, vec,
            pl.BlockSpec((V_HEAD_DIM, HEAD_DIM), lambda b, h, i: (0, 0)),
            pl.BlockSpec((tq_group // tq, None, 2 * HEAD_DIM, tq),
                         lambda b, h, i: (b * groups + i, h, 0, 0)),
            pl.BlockSpec((None, seq, 2 * HEAD_DIM), lambda b, h, i: (b, 0, h)),
            pl.BlockSpec((seq // k_chunk, None, VT_ROWS, k_chunk), lambda b, h, i: (b, h, 0, 0)),
        ],
        out_specs=pl.BlockSpec((None, tq_group, V_HEAD_DIM), lambda b, h, i: (b, i, h)),
        out_shape=jax.ShapeDtypeStruct((batch, seq, V_WIDTH), BF16),
        scratch_shapes=[s_buf, s_buf, stat, stat, p_buf, p_buf, stat, stat, stat,
                        pltpu.VMEM((2, VT_ROWS, tq), F32)],
        compiler_params=_params("parallel", "parallel", "arbitrary"),
        name="diff_attention",
    )(lq1, lk1, lq2, lk2, gain, qt, k, vt)


def _conv_kernel(prev_ref, cur_ref, next_ref, w_ref, b_ref, lg_ref, lb_ref, o_ref, ext_ref,
                 *, row_chunk, norm_chunk):
    i = pl.program_id(1)
    tm = cur_ref.shape[0]
    pad = (CONV_WIDTH - 1) // 2
    ext_ref[0:HALO_ROWS] = jnp.where(i > 0, prev_ref[...], 0.0)
    ext_ref[HALO_ROWS:HALO_ROWS + tm] = cur_ref[...]
    ext_ref[HALO_ROWS + tm:] = jnp.where(i < pl.num_programs(1) - 1, next_ref[...], 0.0)
    bias, ln_g, ln_b = b_ref[...], lg_ref[...], lb_ref[...]

    def channel_sum(x):
        return jnp.sum(jnp.sum(x, axis=2, keepdims=True), axis=1, keepdims=True)

    def conv_body(r, carry):
        r0 = r * row_chunk
        acc = jnp.broadcast_to(bias, (row_chunk,) + bias.shape)
        for t in range(CONV_WIDTH):
            off = HALO_ROWS - pad + t
            acc = acc + w_ref[t] * ext_ref[pl.ds(r0 + off, row_chunk)]
        o_ref[pl.ds(r0, row_chunk)] = acc
        return carry

    lax.fori_loop(0, tm // row_chunk, conv_body, 0)

    def norm_body(r, carry):
        rows = pl.ds(r * norm_chunk, norm_chunk)
        y = o_ref[rows]
        xc = y - channel_sum(y) * (1.0 / CONV_CH)
        var = channel_sum(xc * xc) * (1.0 / CONV_CH)
        y = xc * lax.rsqrt(var + EPS) * ln_g + ln_b
        o_ref[rows] = y * jax.nn.sigmoid(y)
        return carry

    lax.fori_loop(0, tm // norm_chunk, norm_body, 0)


def _halo_specs(tm, seq, width, col_map):
    per_tile = tm // HALO_ROWS
    last = seq // HALO_ROWS - 1
    prev = pl.BlockSpec((None, HALO_ROWS, width),
                        lambda b, i, *f: (b, jnp.maximum(i * per_tile - 1, 0), col_map(*f)))
    nxt = pl.BlockSpec((None, HALO_ROWS, width),
                       lambda b, i, *f: (b, jnp.minimum((i + 1) * per_tile, last), col_map(*f)))
    return prev, nxt


def _conv_module(glu, conv_dw_w, conv_dw_b, conv_ln_g, conv_ln_b):
    batch, seq, _ = glu.shape
    tm = _row_tile(seq)
    tile = (CONV_CH // 128, 128)
    per_tile = tm // HALO_ROWS
    last = seq // HALO_ROWS - 1
    vec = pl.BlockSpec(tile, lambda b, i: (0, 0))
    glu4 = glu.reshape((batch, seq) + tile)
    out = pl.pallas_call(
        functools.partial(_conv_kernel, row_chunk=CONV_ROW_CHUNK, norm_chunk=NORM_ROW_CHUNK),
        grid=(batch, seq // tm),
        in_specs=[
            pl.BlockSpec((None, HALO_ROWS) + tile,
                         lambda b, i: (b, jnp.maximum(i * per_tile - 1, 0), 0, 0)),
            pl.BlockSpec((None, tm) + tile, lambda b, i: (b, i, 0, 0)),
            pl.BlockSpec((None, HALO_ROWS) + tile,
                         lambda b, i: (b, jnp.minimum((i + 1) * per_tile, last), 0, 0)),
            pl.BlockSpec((CONV_WIDTH,) + tile, lambda b, i: (0, 0, 0)),
            vec, vec, vec,
        ],
        out_specs=pl.BlockSpec((None, tm) + tile, lambda b, i: (b, i, 0, 0)),
        out_shape=jax.ShapeDtypeStruct((batch, seq) + tile, F32),
        scratch_shapes=[pltpu.VMEM((tm + 2 * HALO_ROWS,) + tile, F32)],
        compiler_params=_params("parallel", "parallel"),
        name="conv_module",
    )(glu4, glu4, glu4, conv_dw_w.reshape((CONV_WIDTH,) + tile), conv_dw_b.reshape(tile),
      conv_ln_g.reshape(tile), conv_ln_b.reshape(tile))
    return out.reshape(batch, seq, CONV_CH)


def _merge_kernel(o_ref, c_ref, g_ref, wa_ref, wc_ref, m_ref):
    a = jnp.dot(o_ref[...], wa_ref[...], preferred_element_type=F32)
    cb = jnp.dot(c_ref[...].astype(BF16), wc_ref[...], preferred_element_type=F32)
    m = g_ref[:, :D_MODEL].astype(F32) * a + g_ref[:, D_MODEL:].astype(F32) * cb
    m_ref[...] = m.astype(BF16)


def _merge(o2, c2, gates, w_attn_proj, w_conv_proj, tm):
    rows = o2.shape[0]
    return pl.pallas_call(
        _merge_kernel,
        grid=(rows // tm,),
        in_specs=[
            pl.BlockSpec((tm, V_WIDTH), lambda i: (i, 0)),
            pl.BlockSpec((tm, CONV_CH), lambda i: (i, 0)),
            pl.BlockSpec((tm, N_BRANCH * D_MODEL), lambda i: (i, 0)),
            _resident((V_WIDTH, D_MODEL)),
            _resident((CONV_CH, D_MODEL)),
        ],
        out_specs=pl.BlockSpec((tm, D_MODEL), lambda i: (i, 0)),
        out_shape=jax.ShapeDtypeStruct((rows, D_MODEL), BF16),
        compiler_params=_params("parallel"),
        name="merge",
    )(o2, c2, gates, w_attn_proj, w_conv_proj)


def _out_proj_kernel(x_ref, m_ref, w_ref, y_ref):
    y_ref[...] = x_ref[...] + jnp.dot(m_ref[...], w_ref[...], preferred_element_type=F32)


def _out_proj(x2, m2, w_out, tm):
    rows = x2.shape[0]
    return pl.pallas_call(
        _out_proj_kernel,
        grid=(rows // tm,),
        in_specs=[
            pl.BlockSpec((tm, D_MODEL), lambda i: (i, 0)),
            pl.BlockSpec((tm, D_MODEL), lambda i: (i, 0)),
            _resident((D_MODEL, D_MODEL)),
        ],
        out_specs=pl.BlockSpec((tm, D_MODEL), lambda i: (i, 0)),
        out_shape=jax.ShapeDtypeStruct((rows, D_MODEL), F32),
        compiler_params=_params("parallel"),
        name="out_proj",
    )(x2, m2, w_out)


def _ffn_up_kernel(x_ref, g_ref, w_ref, up_ref, h_ref):
    @pl.when(pl.program_id(1) == 0)
    def _():
        x = x_ref[...]
        ms = jnp.mean(x * x, axis=-1, keepdims=True)
        h_ref[...] = (x * lax.rsqrt(ms + EPS) * g_ref[...]).astype(BF16)

    up_ref[...] = jnp.dot(h_ref[...], w_ref[...], preferred_element_type=F32).astype(BF16)


def _ffn_up(x2, norm2_g, w_up, tm):
    rows = x2.shape[0]
    tn = FFN_UP_COL_TILE
    return pl.pallas_call(
        _ffn_up_kernel,
        grid=(rows // tm, 2 * FFN_DIM // tn),
        in_specs=[
            pl.BlockSpec((tm, D_MODEL), lambda i, j: (i, 0)),
            pl.BlockSpec((1, D_MODEL), lambda i, j: (0, 0)),
            pl.BlockSpec((D_MODEL, tn), lambda i, j: (0, j)),
        ],
        out_specs=pl.BlockSpec((tm, tn), lambda i, j: (i, j)),
        out_shape=jax.ShapeDtypeStruct((rows, 2 * FFN_DIM), BF16),
        scratch_shapes=[pltpu.VMEM((tm, D_MODEL), BF16)],
        compiler_params=_params("parallel", "arbitrary"),
        name="ffn_up",
    )(x2, norm2_g, w_up)


def _conv3(prev_ref, cur_ref, next_ref, w_ref, b_ref, first, last):
    cur = cur_ref[...].astype(F32)
    tm = cur.shape[0]
    row = lax.broadcasted_iota(jnp.int32, cur.shape, 0)
    before = jnp.where(first, 0.0, prev_ref[HALO_ROWS - 1:HALO_ROWS, :].astype(F32))
    after = jnp.where(last, 0.0, next_ref[0:1, :].astype(F32))
    up_m1 = jnp.where(row == 0, before, pltpu.roll(cur, 1, 0))
    up_p1 = jnp.where(row == tm - 1, after, pltpu.roll(cur, tm - 1, 0))
    return w_ref[0:1, :] * up_m1 + w_ref[1:2, :] * cur + w_ref[2:3, :] * up_p1 + b_ref[...]


def _ffn_down_kernel(pa_ref, a_ref, na_ref, pb_ref, b_ref, nb_ref, wa_ref, ba_ref, wb_ref, bb_ref,
                     wd_ref, x_ref, y_ref):
    i = pl.program_id(1)
    first = i == 0
    last = i == pl.num_programs(1) - 1
    fa = _conv3(pa_ref, a_ref, na_ref, wa_ref, ba_ref, first, last)
    fb = _conv3(pb_ref, b_ref, nb_ref, wb_ref, bb_ref, first, last)
    act = (fa * jax.nn.sigmoid(fa) * fb).astype(BF16)
    y_ref[...] = x_ref[...] + jnp.dot(act, wd_ref[...], preferred_element_type=F32)


def _ffn_down(up, x1, ffn_dw_w, ffn_dw_b, w_down):
    batch, seq, _ = up.shape
    tm = min(FFN_DOWN_ROW_TILE, seq)
    prev_a, next_a = _halo_specs(tm, seq, FFN_DIM, lambda: 0)
    prev_b, next_b = _halo_specs(tm, seq, FFN_DIM, lambda: 1)
    return pl.pallas_call(
        _ffn_down_kernel,
        grid=(batch, seq // tm),
        in_specs=[
            prev_a, pl.BlockSpec((None, tm, FFN_DIM), lambda b, i: (b, i, 0)), next_a,
            prev_b, pl.BlockSpec((None, tm, FFN_DIM), lambda b, i: (b, i, 1)), next_b,
            pl.BlockSpec((FFN_CONV_WIDTH, FFN_DIM), lambda b, i: (0, 0)),
            pl.BlockSpec((1, FFN_DIM), lambda b, i: (0, 0)),
            pl.BlockSpec((FFN_CONV_WIDTH, FFN_DIM), lambda b, i: (0, 1)),
            pl.BlockSpec((1, FFN_DIM), lambda b, i: (0, 1)),
            _resident((FFN_DIM, D_MODEL)),
            pl.BlockSpec((None, tm, D_MODEL), lambda b, i: (b, i, 0)),
        ],
        out_specs=pl.BlockSpec((None, tm, D_MODEL), lambda b, i: (b, i, 0)),
        out_shape=jax.ShapeDtypeStruct((batch, seq, D_MODEL), F32),
        compiler_params=_params("parallel", "parallel"),
        name="ffn_down",
    )(up, up, up, up, up, up, ffn_dw_w, ffn_dw_b, ffn_dw_w, ffn_dw_b, w_down, x1)


def _rope_tables(seq):
    inv_freq = 1.0 / (ROPE_THETA ** (jnp.arange(0, HEAD_DIM, 2, dtype=F32) / HEAD_DIM))
    ang = jnp.arange(seq, dtype=F32)[:, None] * inv_freq[None, :]
    ang = jnp.concatenate([ang, ang], axis=-1)
    sign = jnp.where(jnp.arange(HEAD_DIM) < HEAD_DIM // 2, -1.0, 1.0).astype(F32)
    return jnp.cos(ang), jnp.sin(ang) * sign


def _encoder_layer(x, layer_idx, p):
    batch, seq, _ = x.shape
    rows = batch * seq
    tm = _row_tile(seq)
    lam_init = 0.8 - 0.6 * math.exp(-0.3 * layer_idx)
    cos, sin_signed = _rope_tables(seq)
    x2 = x.reshape(rows, D_MODEL)

    qt, k, vt, glu, gates = _in_proj(x2, seq, p["norm1_g"], p["w_in"], p["b_gate"], p["q_norm_g"],
                                     p["k_norm_g"], cos, sin_signed)
    o = _attention(qt, k.reshape(batch, seq, QK_WIDTH), vt, batch, seq, p["lambda_q1"],
                   p["lambda_k1"], p["lambda_q2"], p["lambda_k2"], p["subln_g"], lam_init)
    c = _conv_module(glu.reshape(batch, seq, CONV_CH), p["conv_dw_w"], p["conv_dw_b"],
                     p["conv_ln_g"], p["conv_ln_b"])
    m = _merge(o.reshape(rows, V_WIDTH), c.reshape(rows, CONV_CH), gates, p["w_attn_proj"],
               p["w_conv_proj"], tm)
    x1 = _out_proj(x2, m, p["w_out"], tm)
    up = _ffn_up(x1, p["norm2_g"], p["w_up"], tm)
    y = _ffn_down(up.reshape(batch, seq, 2 * FFN_DIM), x1.reshape(batch, seq, D_MODEL),
                  p["ffn_dw_w"], p["ffn_dw_b"], p["w_down"])
    return y


_MATRICES = ("w_in", "w_attn_proj", "w_conv_proj", "w_out", "w_up", "w_down")
_ROW_VECTORS = ("norm1_g", "b_gate", "q_norm_g", "k_norm_g", "lambda_q1", "lambda_k1", "lambda_q2",
                "lambda_k2", "subln_g", "conv_dw_b", "conv_ln_g", "conv_ln_b", "norm2_g", "ffn_dw_b")


def kernel(x_prompt, x_sample, norm1_g, w_in, b_gate, q_norm_g, k_norm_g, lambda_q1, lambda_k1, lambda_q2, lambda_k2, subln_g, w_attn_proj, conv_dw_w, conv_dw_b, conv_ln_g, conv_ln_b, w_conv_proj, w_out, norm2_g, w_up, ffn_dw_w, ffn_dw_b, w_down):
    stacked = dict(norm1_g=norm1_g, w_in=w_in, b_gate=b_gate, q_norm_g=q_norm_g, k_norm_g=k_norm_g,
                   lambda_q1=lambda_q1, lambda_k1=lambda_k1, lambda_q2=lambda_q2,
                   lambda_k2=lambda_k2, subln_g=subln_g, w_attn_proj=w_attn_proj,
                   conv_dw_w=conv_dw_w, conv_dw_b=conv_dw_b, conv_ln_g=conv_ln_g,
                   conv_ln_b=conv_ln_b, w_conv_proj=w_conv_proj, w_out=w_out, norm2_g=norm2_g,
                   w_up=w_up, ffn_dw_w=ffn_dw_w, ffn_dw_b=ffn_dw_b, w_down=w_down)
    depth = w_in.shape[0]
    layers = []
    for l in range(depth):
        p = {}
        for name, value in stacked.items():
            value = value[l]
            if name in _MATRICES:
                value = value.astype(BF16)
            elif name in _ROW_VECTORS:
                value = value.astype(F32)[None, :]
            else:
                value = value.astype(F32)
            p[name] = value
        layers.append(p)

    def run(x):
        for l, p in enumerate(layers):
            x = _encoder_layer(x, l, p)
        return x

    return run(x_prompt), run(x_sample)
```

```python
import functools
import math

import jax
import jax.numpy as jnp
from jax import lax
from jax.experimental import pallas as pl
from jax.experimental.pallas import tpu as pltpu

F32 = jnp.float32
BF16 = jnp.bfloat16

D_MODEL = 2048
N_HEADS = 8
HEAD_DIM = 128
V_HEAD_DIM = 2 * HEAD_DIM
QK_WIDTH = 2 * N_HEADS * HEAD_DIM
V_WIDTH = N_HEADS * V_HEAD_DIM
CONV_CH = 1024
CONV_WIDTH = 31
N_BRANCH = 2
FFN_DIM = 5632
FFN_CONV_WIDTH = 3
ROPE_THETA = 10000.0
EPS = 1e-6
IN_WIDTH = 2 * QK_WIDTH + V_WIDTH + 2 * CONV_CH + N_BRANCH * D_MODEL

VMEM_LIMIT_BYTES = 56 * 1024 * 1024
HALO_ROWS = 16
ROW_TILE = 512
IN_COL_TILE = 2048
FFN_UP_COL_TILE = 2816
FFN_DOWN_ROW_TILE = 256
FFN_DOWN_COL_CHUNK = 256
ATTN_QUERY_GROUP = 4096
CONV_ROW_CHUNK = 16
NORM_ROW_CHUNK = 64
VT_ROWS = V_HEAD_DIM + 16
LOG2E = math.log2(math.e)


def _params(*semantics):
    return pltpu.CompilerParams(dimension_semantics=semantics, vmem_limit_bytes=VMEM_LIMIT_BYTES)


def _resident(shape):
    return pl.BlockSpec(shape, lambda *_: (0,) * len(shape), pipeline_mode=pl.Buffered(1))


def _row_tile(seq):
    return min(ROW_TILE, seq)


def _sigmoid(x):
    return 0.5 * jnp.tanh(0.5 * x) + 0.5


def _qk_epilogue(z, gain, cos, sin_signed, out_scale):
    outs = []
    for h in range(z.shape[1] // HEAD_DIM):
        zh = z[:, h * HEAD_DIM:(h + 1) * HEAD_DIM]
        ms = jnp.mean(zh * zh, axis=-1, keepdims=True)
        y = zh * lax.rsqrt(ms + EPS) * gain
        y = y * cos + pltpu.roll(y, HEAD_DIM // 2, 1) * sin_signed
        outs.append((y * out_scale).astype(BF16))
    return jnp.concatenate(outs, axis=1)


def _q_epilogue_t(z, gain_col, cos_t, sin_t, out_scale):
    tm = z.shape[0]
    gain = jnp.concatenate([gain_col * out_scale] * (tm // HEAD_DIM), axis=1)
    outs = []
    for h in range(z.shape[1] // (2 * HEAD_DIM)):
        zt = z[:, h * 2 * HEAD_DIM:(h + 1) * 2 * HEAD_DIM].T
        subs = []
        for sub in range(2):
            x = zt[sub * HEAD_DIM:(sub + 1) * HEAD_DIM]
            ms = jnp.mean(x * x, axis=0, keepdims=True)
            y = x * lax.rsqrt(ms + EPS) * gain
            swapped = jnp.concatenate([y[HEAD_DIM // 2:], y[:HEAD_DIM // 2]], axis=0)
            subs.append(y * cos_t + swapped * sin_t)
        outs.append(jnp.concatenate(subs, axis=0).astype(BF16))
    return outs


def _in_proj_kernel(x_ref, g1_ref, w_ref, bg_ref, qg_ref, kg_ref, cos_ref, sin_ref, cost_ref,
                    sint_ref, qt_ref, k_ref, vt_ref, glu_ref, gate_ref, h_ref, *, q_scale):
    j = pl.program_id(1)

    @pl.when(j == 0)
    def _():
        x = x_ref[...]
        ms = jnp.mean(x * x, axis=-1, keepdims=True)
        h_ref[...] = (x * lax.rsqrt(ms + EPS) * g1_ref[...]).astype(BF16)

    z = jnp.dot(h_ref[...], w_ref[...], preferred_element_type=F32)

    @pl.when(j == 0)
    def _():
        heads = _q_epilogue_t(z, qg_ref[...], cost_ref[...], sint_ref[...], q_scale)
        for h, head in enumerate(heads):
            qt_ref[h] = head

    @pl.when(j == 1)
    def _():
        k_ref[...] = _qk_epilogue(z, kg_ref[...], cos_ref[...], sin_ref[...], 1.0)

    @pl.when(j == 2)
    def _():
        tm = z.shape[0]
        for h in range(z.shape[1] // V_HEAD_DIM):
            vt_ref[h, :V_HEAD_DIM, :] = z[:, h * V_HEAD_DIM:(h + 1) * V_HEAD_DIM].T.astype(BF16)
            vt_ref[h, V_HEAD_DIM:, :] = jnp.ones((VT_ROWS - V_HEAD_DIM, tm), BF16)

    @pl.when(j == 3)
    def _():
        glu_ref[...] = z[:, :CONV_CH] * _sigmoid(z[:, CONV_CH:])

    @pl.when(j >= 4)
    def _():
        gate_ref[...] = _sigmoid(z + bg_ref[...]).astype(BF16)


def _in_proj(x2, seq, norm1_g, w_in, b_gate, q_norm_g, k_norm_g, cos, sin_signed):
    rows = x2.shape[0]
    tm = _row_tile(seq)
    tn = IN_COL_TILE
    assert tn == QK_WIDTH == V_WIDTH == 2 * CONV_CH == D_MODEL
    seq_tiles = seq // tm
    gate_step0 = (2 * QK_WIDTH + V_WIDTH + 2 * CONV_CH) // tn
    q_gain_col = jnp.broadcast_to(q_norm_g.reshape(HEAD_DIM, 1), (HEAD_DIM, HEAD_DIM))

    def gate_col(j):
        return jnp.clip(j - gate_step0, 0, N_BRANCH - 1)

    kernel = functools.partial(_in_proj_kernel, q_scale=HEAD_DIM ** -0.5 * LOG2E)
    out_shapes = (
        jax.ShapeDtypeStruct((rows // tm, N_HEADS, 2 * HEAD_DIM, tm), BF16),
        jax.ShapeDtypeStruct((rows, QK_WIDTH), BF16),
        jax.ShapeDtypeStruct((rows // tm, N_HEADS, VT_ROWS, tm), BF16),
        jax.ShapeDtypeStruct((rows, CONV_CH), F32),
        jax.ShapeDtypeStruct((rows, N_BRANCH * D_MODEL), BF16),
    )
    return pl.pallas_call(
        kernel,
        grid=(rows // tm, IN_WIDTH // tn),
        in_specs=[
            pl.BlockSpec((tm, D_MODEL), lambda i, j: (i, 0)),
            pl.BlockSpec((1, D_MODEL), lambda i, j: (0, 0)),
            pl.BlockSpec((D_MODEL, tn), lambda i, j: (0, j)),
            pl.BlockSpec((1, tn), lambda i, j: (0, gate_col(j))),
            pl.BlockSpec((HEAD_DIM, HEAD_DIM), lambda i, j: (0, 0)),
            pl.BlockSpec((1, HEAD_DIM), lambda i, j: (0, 0)),
            pl.BlockSpec((tm, HEAD_DIM), lambda i, j: (i % seq_tiles, 0)),
            pl.BlockSpec((tm, HEAD_DIM), lambda i, j: (i % seq_tiles, 0)),
            pl.BlockSpec((HEAD_DIM, tm), lambda i, j: (0, i % seq_tiles)),
            pl.BlockSpec((HEAD_DIM, tm), lambda i, j: (0, i % seq_tiles)),
        ],
        out_specs=(
            pl.BlockSpec((None, N_HEADS, 2 * HEAD_DIM, tm), lambda i, j: (i, 0, 0, 0)),
            pl.BlockSpec((tm, tn), lambda i, j: (i, 0)),
            pl.BlockSpec((None, N_HEADS, VT_ROWS, tm), lambda i, j: (i, 0, 0, 0)),
            pl.BlockSpec((tm, CONV_CH), lambda i, j: (i, 0)),
            pl.BlockSpec((tm, tn), lambda i, j: (i, gate_col(j))),
        ),
        out_shape=out_shapes,
        scratch_shapes=[pltpu.VMEM((tm, D_MODEL), BF16)],
        compiler_params=_params("parallel", "arbitrary"),
        name="in_proj",
    )(x2, norm1_g, w_in, b_gate, q_gain_col, k_norm_g, cos, sin_signed, cos.T, sin_signed.T)


def _attn_kernel(lq1_ref, lk1_ref, lq2_ref, lk2_ref, sg_ref, qt_ref, k_ref, vt_ref, o_ref,
                 s0_ref, s1_ref, x0_ref, x1_ref, p0_ref, p1_ref, a0_ref, a1_ref, m_ref, acc_ref,
                 *, tq, k_chunk, lam_init):
    n_chunks = k_ref.shape[0] // k_chunk
    n_units = qt_ref.shape[0] * n_chunks
    s_bufs, p_bufs = (s0_ref, s1_ref), (p0_ref, p1_ref)
    a_bufs, max_bufs = (a0_ref, a1_ref), (x0_ref, x1_ref)

    def scores(u, slot):
        c0 = pl.multiple_of((u % n_chunks) * k_chunk, k_chunk)
        for sub in range(2):
            dims = slice(sub * HEAD_DIM, (sub + 1) * HEAD_DIM)
            st = jnp.dot(k_ref[pl.ds(c0, k_chunk), dims], qt_ref[u // n_chunks, dims, :],
                         preferred_element_type=F32)
            s_bufs[slot][sub] = st
            max_bufs[slot][sub] = jnp.broadcast_to(jnp.max(st, axis=0, keepdims=True), (8, tq))

    def softmax(u, slot):
        m_old = m_ref[...]
        if slot == 0:
            m_old = m_old + jnp.where(u % n_chunks == 0, -jnp.inf, 0.0)
        m_new = jnp.maximum(m_old, max_bufs[slot][...])
        a_bufs[slot][...] = jnp.exp2(m_old - m_new)
        m_ref[...] = m_new
        for sub in range(2):
            d = s_bufs[slot][sub] - m_new[sub, 0:1, :]
            p_bufs[slot][sub] = jnp.exp2(d).astype(BF16)

    def values(u, slot):
        vt = vt_ref[u % n_chunks]
        for sub in range(2):
            pv = jnp.dot(vt, p_bufs[slot][sub], preferred_element_type=F32)
            acc_ref[sub] = a_bufs[slot][sub, 0:1, :] * acc_ref[sub] + pv

    def normalise(tile):
        lam = (jnp.exp(jnp.sum(lq1_ref[...] * lk1_ref[...], axis=-1, keepdims=True))
               - jnp.exp(jnp.sum(lq2_ref[...] * lk2_ref[...], axis=-1, keepdims=True)) + lam_init)
        o0 = acc_ref[0, :V_HEAD_DIM, :] / acc_ref[0, V_HEAD_DIM:V_HEAD_DIM + 1, :]
        o1 = acc_ref[1, :V_HEAD_DIM, :] / acc_ref[1, V_HEAD_DIM:V_HEAD_DIM + 1, :]
        o = o0 - lam * o1
        ms = jnp.mean(o * o, axis=0, keepdims=True)
        gain = jnp.concatenate([sg_ref[...]] * (tq // HEAD_DIM), axis=1)
        o = o * lax.rsqrt(ms + EPS) * gain * (1.0 - lam_init)
        o_ref[pl.ds(pl.multiple_of(tile * tq, tq), tq), :] = o.T.astype(BF16)

    m_ref[...] = jnp.full(m_ref.shape, -jnp.inf, F32)
    acc_ref[...] = jnp.zeros(acc_ref.shape, F32)

    scores(0, 0)
    scores(1, 1)
    softmax(0, 0)

    def body(t, carry):
        u = 2 * t + 2
        scores(u, 0)
        softmax(u - 1, 1)
        values(u - 2, 0)
        scores(u + 1, 1)
        softmax(u, 0)
        values(u - 1, 1)

        @pl.when(u % n_chunks == 0)
        def _():
            normalise(u // n_chunks - 1)

        return carry

    lax.fori_loop(0, n_units // 2 - 1, body, 0)
    softmax(n_units - 1, 1)
    values(n_units - 2, 0)
    values(n_units - 1, 1)
    normalise(n_units // n_chunks - 1)


def _attention(qt, k, vt, batch, seq, lq1, lk1, lq2, lk2, subln_g, lam_init):
    tq = qt.shape[-1]
    k_chunk = vt.shape[-1]
    tq_group = min(ATTN_QUERY_GROUP, seq)
    groups = seq // tq_group
    assert (seq // k_chunk) % 2 == 0
    gain = jnp.broadcast_to(subln_g.reshape(V_HEAD_DIM, 1), (V_HEAD_DIM, HEAD_DIM))
    vec = pl.BlockSpec((1, HEAD_DIM), lambda b, h, i: (0, 0))
    kernel = functools.partial(_attn_kernel, tq=tq, k_chunk=k_chunk, lam_init=lam_init)
    s_buf = pltpu.VMEM((2, k_chunk, tq), F32)
    p_buf = pltpu.VMEM((2, k_chunk, tq), BF16)
    stat = pltpu.VMEM((2, 8, tq), F32)
    return pl.pallas_call(
        kernel,
        grid=(batch, N_HEADS, groups),
        in_specs=[
            vec, vec, vec, vec,
            pl.BlockSpec((V_HEAD_DIM, HEAD_DIM), lambda b, h, i: (0, 0)),
            pl.BlockSpec((tq_group // tq, None, 2 * HEAD_DIM, tq),
                         lambda b, h, i: (b * groups + i, h, 0, 0)),
            pl.BlockSpec((None, seq, 2 * HEAD_DIM), lambda b, h, i: (b, 0, h)),
            pl.BlockSpec((seq // k_chunk, None, VT_ROWS, k_chunk), lambda b, h, i: (b, h, 0, 0)),
        ],
        out_specs=pl.BlockSpec((None, tq_group, V_HEAD_DIM), lambda b, h, i: (b, i, h)),
        out_shape=jax.ShapeDtypeStruct((batch, seq, V_WIDTH), BF16),
        scratch_shapes=[s_buf, s_buf, stat, stat, p_buf, p_buf, stat, stat, stat,
                        pltpu.VMEM((2, VT_ROWS, tq), F32)],
        compiler_params=_params("parallel", "parallel", "arbitrary"),
        name="diff_attention",
    )(lq1, lk1, lq2, lk2, gain, qt, k, vt)


def _conv_kernel(prev_ref, cur_ref, next_ref, w_ref, b_ref, lg_ref, lb_ref, o_ref, ext_ref,
                 *, row_chunk, norm_chunk):
    i = pl.program_id(1)
    tm = cur_ref.shape[0]
    pad = (CONV_WIDTH - 1) // 2
    ext_ref[0:HALO_ROWS] = jnp.where(i > 0, prev_ref[...], 0.0)
    ext_ref[HALO_ROWS:HALO_ROWS + tm] = cur_ref[...]
    ext_ref[HALO_ROWS + tm:] = jnp.where(i < pl.num_programs(1) - 1, next_ref[...], 0.0)
    bias, ln_g, ln_b = b_ref[...], lg_ref[...], lb_ref[...]

    def channel_sum(x):
        return jnp.sum(jnp.sum(x, axis=2, keepdims=True), axis=1, keepdims=True)

    def conv_body(r, carry):
        r0 = r * row_chunk
        acc = jnp.broadcast_to(bias, (row_chunk,) + bias.shape)
        for t in range(CONV_WIDTH):
            off = HALO_ROWS - pad + t
            acc = acc + w_ref[t] * ext_ref[pl.ds(r0 + off, row_chunk)]
        o_ref[pl.ds(r0, row_chunk)] = acc
        return carry

    lax.fori_loop(0, tm // row_chunk, conv_body, 0)

    def norm_body(r, carry):
        rows = pl.ds(r * norm_chunk, norm_chunk)
        y = o_ref[rows]
        xc = y - channel_sum(y) * (1.0 / CONV_CH)
        var = channel_sum(xc * xc) * (1.0 / CONV_CH)
        y = xc * lax.rsqrt(var + EPS) * ln_g + ln_b
        o_ref[rows] = y * jax.nn.sigmoid(y)
        return carry

    lax.fori_loop(0, tm // norm_chunk, norm_body, 0)


def _halo_specs(tm, seq, width, col_map):
    per_tile = tm // HALO_ROWS
    last = seq // HALO_ROWS - 1
    prev = pl.BlockSpec((None, HALO_ROWS, width),
                        lambda b, i, *f: (b, jnp.maximum(i * per_tile - 1, 0), col_map(*f)))
    nxt = pl.BlockSpec((None, HALO_ROWS, width),
                       lambda b, i, *f: (b, jnp.minimum((i + 1) * per_tile, last), col_map(*f)))
    return prev, nxt


def _conv_module(glu, conv_dw_w, conv_dw_b, conv_ln_g, conv_ln_b):
    batch, seq, _ = glu.shape
    tm = _row_tile(seq)
    tile = (CONV_CH // 128, 128)
    per_tile = tm // HALO_ROWS
    last = seq // HALO_ROWS - 1
    vec = pl.BlockSpec(tile, lambda b, i: (0, 0))
    glu4 = glu.reshape((batch, seq) + tile)
    out = pl.pallas_call(
        functools.partial(_conv_kernel, row_chunk=CONV_ROW_CHUNK, norm_chunk=NORM_ROW_CHUNK),
        grid=(batch, seq // tm),
        in_specs=[
            pl.BlockSpec((None, HALO_ROWS) + tile,
                         lambda b, i: (b, jnp.maximum(i * per_tile - 1, 0), 0, 0)),
            pl.BlockSpec((None, tm) + tile, lambda b, i: (b, i, 0, 0)),
            pl.BlockSpec((None, HALO_ROWS) + tile,
                         lambda b, i: (b, jnp.minimum((i + 1) * per_tile, last), 0, 0)),
            pl.BlockSpec((CONV_WIDTH,) + tile, lambda b, i: (0, 0, 0)),
            vec, vec, vec,
        ],
        out_specs=pl.BlockSpec((None, tm) + tile, lambda b, i: (b, i, 0, 0)),
        out_shape=jax.ShapeDtypeStruct((batch, seq) + tile, F32),
        scratch_shapes=[pltpu.VMEM((tm + 2 * HALO_ROWS,) + tile, F32)],
        compiler_params=_params("parallel", "parallel"),
        name="conv_module",
    )(glu4, glu4, glu4, conv_dw_w.reshape((CONV_WIDTH,) + tile), conv_dw_b.reshape(tile),
      conv_ln_g.reshape(tile), conv_ln_b.reshape(tile))
    return out.reshape(batch, seq, CONV_CH)


def _merge_kernel(o_ref, c_ref, g_ref, wa_ref, wc_ref, m_ref):
    a = jnp.dot(o_ref[...], wa_ref[...], preferred_element_type=F32)
    cb = jnp.dot(c_ref[...].astype(BF16), wc_ref[...], preferred_element_type=F32)
    m = g_ref[:, :D_MODEL].astype(F32) * a + g_ref[:, D_MODEL:].astype(F32) * cb
    m_ref[...] = m.astype(BF16)


def _merge(o2, c2, gates, w_attn_proj, w_conv_proj, tm):
    rows = o2.shape[0]
    return pl.pallas_call(
        _merge_kernel,
        grid=(rows // tm,),
        in_specs=[
            pl.BlockSpec((tm, V_WIDTH), lambda i: (i, 0)),
            pl.BlockSpec((tm, CONV_CH), lambda i: (i, 0)),
            pl.BlockSpec((tm, N_BRANCH * D_MODEL), lambda i: (i, 0)),
            _resident((V_WIDTH, D_MODEL)),
            _resident((CONV_CH, D_MODEL)),
        ],
        out_specs=pl.BlockSpec((tm, D_MODEL), lambda i: (i, 0)),
        out_shape=jax.ShapeDtypeStruct((rows, D_MODEL), BF16),
        compiler_params=_params("parallel"),
        name="merge",
    )(o2, c2, gates, w_attn_proj, w_conv_proj)


def _out_proj_kernel(x_ref, m_ref, w_ref, y_ref):
    y_ref[...] = x_ref[...] + jnp.dot(m_ref[...], w_ref[...], preferred_element_type=F32)


def _out_proj(x2, m2, w_out, tm):
    rows = x2.shape[0]
    return pl.pallas_call(
        _out_proj_kernel,
        grid=(rows // tm,),
        in_specs=[
            pl.BlockSpec((tm, D_MODEL), lambda i: (i, 0)),
            pl.BlockSpec((tm, D_MODEL), lambda i: (i, 0)),
            _resident((D_MODEL, D_MODEL)),
        ],
        out_specs=pl.BlockSpec((tm, D_MODEL), lambda i: (i, 0)),
        out_shape=jax.ShapeDtypeStruct((rows, D_MODEL), F32),
        compiler_params=_params("parallel"),
        name="out_proj",
    )(x2, m2, w_out)


def _ffn_up_kernel(x_ref, g_ref, w_ref, up_ref, h_ref):
    @pl.when(pl.program_id(1) == 0)
    def _():
        x = x_ref[...]
        ms = jnp.mean(x * x, axis=-1, keepdims=True)
        h_ref[...] = (x * lax.rsqrt(ms + EPS) * g_ref[...]).astype(BF16)

    up_ref[...] = jnp.dot(h_ref[...], w_ref[...], preferred_element_type=F32).astype(BF16)


def _ffn_up(x2, norm2_g, w_up, tm):
    rows = x2.shape[0]
    tn = FFN_UP_COL_TILE
    return pl.pallas_call(
        _ffn_up_kernel,
        grid=(rows // tm, 2 * FFN_DIM // tn),
        in_specs=[
            pl.BlockSpec((tm, D_MODEL), lambda i, j: (i, 0)),
            pl.BlockSpec((1, D_MODEL), lambda i, j: (0, 0)),
            pl.BlockSpec((D_MODEL, tn), lambda i, j: (0, j)),
        ],
        out_specs=pl.BlockSpec((tm, tn), lambda i, j: (i, j)),
        out_shape=jax.ShapeDtypeStruct((rows, 2 * FFN_DIM), BF16),
        scratch_shapes=[pltpu.VMEM((tm, D_MODEL), BF16)],
        compiler_params=_params("parallel", "arbitrary"),
        name="ffn_up",
    )(x2, norm2_g, w_up)


def _conv3(prev_ref, cur_ref, next_ref, w_ref, b_ref, first, last, cols):
    cur = cur_ref[:, cols].astype(F32)
    tm = cur.shape[0]
    row = lax.broadcasted_iota(jnp.int32, cur.shape, 0)
    before = jnp.where(first, 0.0, prev_ref[HALO_ROWS - 1:HALO_ROWS, cols].astype(F32))
    after = jnp.where(last, 0.0, next_ref[0:1, cols].astype(F32))
    up_m1 = jnp.where(row == 0, before, pltpu.roll(cur, 1, 0))
    up_p1 = jnp.where(row == tm - 1, after, pltpu.roll(cur, tm - 1, 0))
    return (w_ref[0:1, cols] * up_m1 + w_ref[1:2, cols] * cur + w_ref[2:3, cols] * up_p1
            + b_ref[:, cols])


def _ffn_down_kernel(pa_ref, a_ref, na_ref, pb_ref, b_ref, nb_ref, wa_ref, ba_ref, wb_ref, bb_ref,
                     wd_ref, x_ref, y_ref):
    i = pl.program_id(1)
    first = i == 0
    last = i == pl.num_programs(1) - 1
    y = x_ref[...]
    for chunk in range(FFN_DIM // FFN_DOWN_COL_CHUNK):
        cols = slice(chunk * FFN_DOWN_COL_CHUNK, (chunk + 1) * FFN_DOWN_COL_CHUNK)
        fa = _conv3(pa_ref, a_ref, na_ref, wa_ref, ba_ref, first, last, cols)
        fb = _conv3(pb_ref, b_ref, nb_ref, wb_ref, bb_ref, first, last, cols)
        act = (fa * jax.nn.sigmoid(fa) * fb).astype(BF16)
        y = y + jnp.dot(act, wd_ref[cols, :], preferred_element_type=F32)
    y_ref[...] = y


def _ffn_down(up, x1, ffn_dw_w, ffn_dw_b, w_down):
    batch, seq, _ = up.shape
    tm = min(FFN_DOWN_ROW_TILE, seq)
    prev_a, next_a = _halo_specs(tm, seq, FFN_DIM, lambda: 0)
    prev_b, next_b = _halo_specs(tm, seq, FFN_DIM, lambda: 1)
    return pl.pallas_call(
        _ffn_down_kernel,
        grid=(batch, seq // tm),
        in_specs=[
            prev_a, pl.BlockSpec((None, tm, FFN_DIM), lambda b, i: (b, i, 0)), next_a,
            prev_b, pl.BlockSpec((None, tm, FFN_DIM), lambda b, i: (b, i, 1)), next_b,
            pl.BlockSpec((FFN_CONV_WIDTH, FFN_DIM), lambda b, i: (0, 0)),
            pl.BlockSpec((1, FFN_DIM), lambda b, i: (0, 0)),
            pl.BlockSpec((FFN_CONV_WIDTH, FFN_DIM), lambda b, i: (0, 1)),
            pl.BlockSpec((1, FFN_DIM), lambda b, i: (0, 1)),
            _resident((FFN_DIM, D_MODEL)),
            pl.BlockSpec((None, tm, D_MODEL), lambda b, i: (b, i, 0)),
        ],
        out_specs=pl.BlockSpec((None, tm, D_MODEL), lambda b, i: (b, i, 0)),
        out_shape=jax.ShapeDtypeStruct((batch, seq, D_MODEL), F32),
        compiler_params=_params("parallel", "parallel"),
        name="ffn_down",
    )(up, up, up, up, up, up, ffn_dw_w, ffn_dw_b, ffn_dw_w, ffn_dw_b, w_down, x1)


def _rope_tables(seq):
    inv_freq = 1.0 / (ROPE_THETA ** (jnp.arange(0, HEAD_DIM, 2, dtype=F32) / HEAD_DIM))
    ang = jnp.arange(seq, dtype=F32)[:, None] * inv_freq[None, :]
    ang = jnp.concatenate([ang, ang], axis=-1)
    sign = jnp.where(jnp.arange(HEAD_DIM) < HEAD_DIM // 2, -1.0, 1.0).astype(F32)
    return jnp.cos(ang), jnp.sin(ang) * sign


def _encoder_layer(x, layer_idx, p):
    batch, seq, _ = x.shape
    rows = batch * seq
    tm = _row_tile(seq)
    lam_init = 0.8 - 0.6 * math.exp(-0.3 * layer_idx)
    cos, sin_signed = _rope_tables(seq)
    x2 = x.reshape(rows, D_MODEL)

    qt, k, vt, glu, gates = _in_proj(x2, seq, p["norm1_g"], p["w_in"], p["b_gate"], p["q_norm_g"],
                                     p["k_norm_g"], cos, sin_signed)
    o = _attention(qt, k.reshape(batch, seq, QK_WIDTH), vt, batch, seq, p["lambda_q1"],
                   p["lambda_k1"], p["lambda_q2"], p["lambda_k2"], p["subln_g"], lam_init)
    c = _conv_module(glu.reshape(batch, seq, CONV_CH), p["conv_dw_w"], p["conv_dw_b"],
                     p["conv_ln_g"], p["conv_ln_b"])
    m = _merge(o.reshape(rows, V_WIDTH), c.reshape(rows, CONV_CH), gates, p["w_attn_proj"],
               p["w_conv_proj"], tm)
    x1 = _out_proj(x2, m, p["w_out"], tm)
    up = _ffn_up(x1, p["norm2_g"], p["w_up"], tm)
    y = _ffn_down(up.reshape(batch, seq, 2 * FFN_DIM), x1.reshape(batch, seq, D_MODEL),
                  p["ffn_dw_w"], p["ffn_dw_b"], p["w_down"])
    return y


_MATRICES = ("w_in", "w_attn_proj", "w_conv_proj", "w_out", "w_up", "w_down")
_ROW_VECTORS = ("norm1_g", "b_gate", "q_norm_g", "k_norm_g", "lambda_q1", "lambda_k1", "lambda_q2",
                "lambda_k2", "subln_g", "conv_dw_b", "conv_ln_g", "conv_ln_b", "norm2_g", "ffn_dw_b")


def kernel(x_prompt, x_sample, norm1_g, w_in, b_gate, q_norm_g, k_norm_g, lambda_q1, lambda_k1, lambda_q2, lambda_k2, subln_g, w_attn_proj, conv_dw_w, conv_dw_b, conv_ln_g, conv_ln_b, w_conv_proj, w_out, norm2_g, w_up, ffn_dw_w, ffn_dw_b, w_down):
    stacked = dict(norm1_g=norm1_g, w_in=w_in, b_gate=b_gate, q_norm_g=q_norm_g, k_norm_g=k_norm_g,
                   lambda_q1=lambda_q1, lambda_k1=lambda_k1, lambda_q2=lambda_q2,
                   lambda_k2=lambda_k2, subln_g=subln_g, w_attn_proj=w_attn_proj,
                   conv_dw_w=conv_dw_w, conv_dw_b=conv_dw_b, conv_ln_g=conv_ln_g,
                   conv_ln_b=conv_ln_b, w_conv_proj=w_conv_proj, w_out=w_out, norm2_g=norm2_g,
                   w_up=w_up, ffn_dw_w=ffn_dw_w, ffn_dw_b=ffn_dw_b, w_down=w_down)
    depth = w_in.shape[0]
    layers = []
    for l in range(depth):
        p = {}
        for name, value in stacked.items():
            value = value[l]
            if name in _MATRICES:
                value = value.astype(BF16)
            elif name in _ROW_VECTORS:
                value = value.astype(F32)[None, :]
            else:
                value = value.astype(F32)
            p[name] = value
        layers.append(p)

    def run(x):
        for l, p in enumerate(layers):
            x = _encoder_layer(x, l, p)
        return x

    return run(x_prompt), run(x_sample)
```

```python
import functools
import math

import jax
import jax.numpy as jnp
from jax import lax
from jax.experimental import pallas as pl
from jax.experimental.pallas import tpu as pltpu

F32 = jnp.float32
BF16 = jnp.bfloat16

D_MODEL = 2048
N_HEADS = 8
HEAD_DIM = 128
V_HEAD_DIM = 2 * HEAD_DIM
QK_WIDTH = 2 * N_HEADS * HEAD_DIM
V_WIDTH = N_HEADS * V_HEAD_DIM
CONV_CH = 1024
CONV_WIDTH = 31
N_BRANCH = 2
FFN_DIM = 5632
FFN_CONV_WIDTH = 3
ROPE_THETA = 10000.0
EPS = 1e-6
IN_WIDTH = 2 * QK_WIDTH + V_WIDTH + 2 * CONV_CH + N_BRANCH * D_MODEL

VMEM_LIMIT_BYTES = 56 * 1024 * 1024
HALO_ROWS = 16
ROW_TILE = 512
IN_COL_TILE = 2048
FFN_UP_COL_TILE = 2816
FFN_DOWN_ROW_TILE = 256
ATTN_QUERY_GROUP = 4096
CONV_ROW_CHUNK = 32
NORM_ROW_CHUNK = 128
VT_ROWS = V_HEAD_DIM + 16
LOG2E = math.log2(math.e)


def _params(*semantics):
    return pltpu.CompilerParams(dimension_semantics=semantics, vmem_limit_bytes=VMEM_LIMIT_BYTES)


def _resident(shape):
    return pl.BlockSpec(shape, lambda *_: (0,) * len(shape), pipeline_mode=pl.Buffered(1))


def _row_tile(seq):
    return min(ROW_TILE, seq)


def _sigmoid(x):
    return 0.5 * jnp.tanh(0.5 * x) + 0.5


def _qk_epilogue(z, gain, cos, sin_signed, out_scale):
    outs = []
    for h in range(z.shape[1] // HEAD_DIM):
        zh = z[:, h * HEAD_DIM:(h + 1) * HEAD_DIM]
        ms = jnp.mean(zh * zh, axis=-1, keepdims=True)
        y = zh * lax.rsqrt(ms + EPS) * gain
        y = y * cos + pltpu.roll(y, HEAD_DIM // 2, 1) * sin_signed
        outs.append((y * out_scale).astype(BF16))
    return jnp.concatenate(outs, axis=1)


def _q_epilogue_t(z, gain_col, cos_t, sin_t, out_scale):
    tm = z.shape[0]
    gain = jnp.concatenate([gain_col * out_scale] * (tm // HEAD_DIM), axis=1)
    outs = []
    for h in range(z.shape[1] // (2 * HEAD_DIM)):
        zt = z[:, h * 2 * HEAD_DIM:(h + 1) * 2 * HEAD_DIM].T
        subs = []
        for sub in range(2):
            x = zt[sub * HEAD_DIM:(sub + 1) * HEAD_DIM]
            ms = jnp.mean(x * x, axis=0, keepdims=True)
            y = x * lax.rsqrt(ms + EPS) * gain
            swapped = jnp.concatenate([y[HEAD_DIM // 2:], y[:HEAD_DIM // 2]], axis=0)
            subs.append(y * cos_t + swapped * sin_t)
        outs.append(jnp.concatenate(subs, axis=0).astype(BF16))
    return outs


def _in_proj_kernel(x_ref, g1_ref, w_ref, bg_ref, qg_ref, kg_ref, cos_ref, sin_ref, cost_ref,
                    sint_ref, qt_ref, k_ref, vt_ref, glu_ref, gate_ref, h_ref, *, q_scale):
    j = pl.program_id(1)

    @pl.when(j == 0)
    def _():
        x = x_ref[...]
        ms = jnp.mean(x * x, axis=-1, keepdims=True)
        h_ref[...] = (x * lax.rsqrt(ms + EPS) * g1_ref[...]).astype(BF16)

    z = jnp.dot(h_ref[...], w_ref[...], preferred_element_type=F32)

    @pl.when(j == 0)
    def _():
        heads = _q_epilogue_t(z, qg_ref[...], cost_ref[...], sint_ref[...], q_scale)
        for h, head in enumerate(heads):
            qt_ref[h] = head

    @pl.when(j == 1)
    def _():
        k_ref[...] = _qk_epilogue(z, kg_ref[...], cos_ref[...], sin_ref[...], 1.0)

    @pl.when(j == 2)
    def _():
        tm = z.shape[0]
        for h in range(z.shape[1] // V_HEAD_DIM):
            vt_ref[h, :V_HEAD_DIM, :] = z[:, h * V_HEAD_DIM:(h + 1) * V_HEAD_DIM].T.astype(BF16)
            vt_ref[h, V_HEAD_DIM:, :] = jnp.ones((VT_ROWS - V_HEAD_DIM, tm), BF16)

    @pl.when(j == 3)
    def _():
        glu_ref[...] = z[:, :CONV_CH] * _sigmoid(z[:, CONV_CH:])

    @pl.when(j >= 4)
    def _():
        gate_ref[...] = _sigmoid(z + bg_ref[...]).astype(BF16)


def _in_proj(x2, seq, norm1_g, w_in, b_gate, q_norm_g, k_norm_g, cos, sin_signed):
    rows = x2.shape[0]
    tm = _row_tile(seq)
    tn = IN_COL_TILE
    assert tn == QK_WIDTH == V_WIDTH == 2 * CONV_CH == D_MODEL
    seq_tiles = seq // tm
    gate_step0 = (2 * QK_WIDTH + V_WIDTH + 2 * CONV_CH) // tn
    q_gain_col = jnp.broadcast_to(q_norm_g.reshape(HEAD_DIM, 1), (HEAD_DIM, HEAD_DIM))

    def gate_col(j):
        return jnp.clip(j - gate_step0, 0, N_BRANCH - 1)

    kernel = functools.partial(_in_proj_kernel, q_scale=HEAD_DIM ** -0.5 * LOG2E)
    out_shapes = (
        jax.ShapeDtypeStruct((rows // tm, N_HEADS, 2 * HEAD_DIM, tm), BF16),
        jax.ShapeDtypeStruct((rows, QK_WIDTH), BF16),
        jax.ShapeDtypeStruct((rows // tm, N_HEADS, VT_ROWS, tm), BF16),
        jax.ShapeDtypeStruct((rows, CONV_CH), F32),
        jax.ShapeDtypeStruct((rows, N_BRANCH * D_MODEL), BF16),
    )
    return pl.pallas_call(
        kernel,
        grid=(rows // tm, IN_WIDTH // tn),
        in_specs=[
            pl.BlockSpec((tm, D_MODEL), lambda i, j: (i, 0)),
            pl.BlockSpec((1, D_MODEL), lambda i, j: (0, 0)),
            pl.BlockSpec((D_MODEL, tn), lambda i, j: (0, j)),
            pl.BlockSpec((1, tn), lambda i, j: (0, gate_col(j))),
            pl.BlockSpec((HEAD_DIM, HEAD_DIM), lambda i, j: (0, 0)),
            pl.BlockSpec((1, HEAD_DIM), lambda i, j: (0, 0)),
            pl.BlockSpec((tm, HEAD_DIM), lambda i, j: (i % seq_tiles, 0)),
            pl.BlockSpec((tm, HEAD_DIM), lambda i, j: (i % seq_tiles, 0)),
            pl.BlockSpec((HEAD_DIM, tm), lambda i, j: (0, i % seq_tiles)),
            pl.BlockSpec((HEAD_DIM, tm), lambda i, j: (0, i % seq_tiles)),
        ],
        out_specs=(
            pl.BlockSpec((None, N_HEADS, 2 * HEAD_DIM, tm), lambda i, j: (i, 0, 0, 0)),
            pl.BlockSpec((tm, tn), lambda i, j: (i, 0)),
            pl.BlockSpec((None, N_HEADS, VT_ROWS, tm), lambda i, j: (i, 0, 0, 0)),
            pl.BlockSpec((tm, CONV_CH), lambda i, j: (i, 0)),
            pl.BlockSpec((tm, tn), lambda i, j: (i, gate_col(j))),
        ),
        out_shape=out_shapes,
        scratch_shapes=[pltpu.VMEM((tm, D_MODEL), BF16)],
        compiler_params=_params("parallel", "arbitrary"),
        name="in_proj",
    )(x2, norm1_g, w_in, b_gate, q_gain_col, k_norm_g, cos, sin_signed, cos.T, sin_signed.T)


def _attn_kernel(lq1_ref, lk1_ref, lq2_ref, lk2_ref, sg_ref, qt_ref, k_ref, vt_ref, o_ref,
                 s0_ref, s1_ref, x0_ref, x1_ref, p0_ref, p1_ref, a0_ref, a1_ref, m_ref, acc_ref,
                 *, tq, k_chunk, lam_init):
    n_chunks = k_ref.shape[0] // k_chunk
    n_units = qt_ref.shape[0] * n_chunks
    s_bufs, p_bufs = (s0_ref, s1_ref), (p0_ref, p1_ref)
    a_bufs, max_bufs = (a0_ref, a1_ref), (x0_ref, x1_ref)

    def scores(u, slot):
        c0 = pl.multiple_of((u % n_chunks) * k_chunk, k_chunk)
        for sub in range(2):
            dims = slice(sub * HEAD_DIM, (sub + 1) * HEAD_DIM)
            st = jnp.dot(k_ref[pl.ds(c0, k_chunk), dims], qt_ref[u // n_chunks, dims, :],
                         preferred_element_type=F32)
            s_bufs[slot][sub] = st
            max_bufs[slot][sub] = jnp.broadcast_to(jnp.max(st, axis=0, keepdims=True), (8, tq))

    def softmax(u, slot):
        m_old = m_ref[...]
        if slot == 0:
            m_old = m_old + jnp.where(u % n_chunks == 0, -jnp.inf, 0.0)
        m_new = jnp.maximum(m_old, max_bufs[slot][...])
        a_bufs[slot][...] = jnp.exp2(m_old - m_new)
        m_ref[...] = m_new
        for sub in range(2):
            d = s_bufs[slot][sub] - m_new[sub, 0:1, :]
            p_bufs[slot][sub] = jnp.exp2(d).astype(BF16)

    def values(u, slot):
        vt = vt_ref[u % n_chunks]
        for sub in range(2):
            pv = jnp.dot(vt, p_bufs[slot][sub], preferred_element_type=F32)
            acc_ref[sub] = a_bufs[slot][sub, 0:1, :] * acc_ref[sub] + pv

    def normalise(tile):
        lam = (jnp.exp(jnp.sum(lq1_ref[...] * lk1_ref[...], axis=-1, keepdims=True))
               - jnp.exp(jnp.sum(lq2_ref[...] * lk2_ref[...], axis=-1, keepdims=True)) + lam_init)
        o0 = acc_ref[0, :V_HEAD_DIM, :] / acc_ref[0, V_HEAD_DIM:V_HEAD_DIM + 1, :]
        o1 = acc_ref[1, :V_HEAD_DIM, :] / acc_ref[1, V_HEAD_DIM:V_HEAD_DIM + 1, :]
        o = o0 - lam * o1
        ms = jnp.mean(o * o, axis=0, keepdims=True)
        gain = jnp.concatenate([sg_ref[...]] * (tq // HEAD_DIM), axis=1)
        o = o * lax.rsqrt(ms + EPS) * gain * (1.0 - lam_init)
        o_ref[pl.ds(pl.multiple_of(tile * tq, tq), tq), :] = o.T.astype(BF16)

    m_ref[...] = jnp.full(m_ref.shape, -jnp.inf, F32)
    acc_ref[...] = jnp.zeros(acc_ref.shape, F32)

    scores(0, 0)
    scores(1, 1)
    softmax(0, 0)

    def body(t, carry):
        u = 2 * t + 2
        scores(u, 0)
        softmax(u - 1, 1)
        values(u - 2, 0)
        scores(u + 1, 1)
        softmax(u, 0)
        values(u - 1, 1)

        @pl.when(u % n_chunks == 0)
        def _():
            normalise(u // n_chunks - 1)

        return carry

    lax.fori_loop(0, n_units // 2 - 1, body, 0)
    softmax(n_units - 1, 1)
    values(n_units - 2, 0)
    values(n_units - 1, 1)
    normalise(n_units // n_chunks - 1)


def _attention(qt, k, vt, batch, seq, lq1, lk1, lq2, lk2, subln_g, lam_init):
    tq = qt.shape[-1]
    k_chunk = vt.shape[-1]
    tq_group = min(ATTN_QUERY_GROUP, seq)
    groups = seq // tq_group
    assert (seq // k_chunk) % 2 == 0
    gain = jnp.broadcast_to(subln_g.reshape(V_HEAD_DIM, 1), (V_HEAD_DIM, HEAD_DIM))
    vec = pl.BlockSpec((1, HEAD_DIM), lambda b, h, i: (0, 0))
    kernel = functools.partial(_attn_kernel, tq=tq, k_chunk=k_chunk, lam_init=lam_init)
    s_buf = pltpu.VMEM((2, k_chunk, tq), F32)
    p_buf = pltpu.VMEM((2, k_chunk, tq), BF16)
    stat = pltpu.VMEM((2, 8, tq), F32)
    return pl.pallas_call(
        kernel,
        grid=(batch, N_HEADS, groups),
        in_specs=[
            vec, vec, vec, vec,
            pl.BlockSpec((V_HEAD_DIM, HEAD_DIM), lambda b, h, i: (0, 0)),
            pl.BlockSpec((tq_group // tq, None, 2 * HEAD_DIM, tq),
                         lambda b, h, i: (b * groups + i, h, 0, 0)),
            pl.BlockSpec((None, seq, 2 * HEAD_DIM), lambda b, h, i: (b, 0, h)),
            pl.BlockSpec((seq // k_chunk, None, VT_ROWS, k_chunk), lambda b, h, i: (b, h, 0, 0)),
        ],
        out_specs=pl.BlockSpec((None, tq_group, V_HEAD_DIM), lambda b, h, i: (b, i, h)),
        out_shape=jax.ShapeDtypeStruct((batch, seq, V_WIDTH), BF16),
        scratch_shapes=[s_buf, s_buf, stat, stat, p_buf, p_buf, stat, stat, stat,
                        pltpu.VMEM((2, VT_ROWS, tq), F32)],
        compiler_params=_params("parallel", "parallel", "arbitrary"),
        name="diff_attention",
    )(lq1, lk1, lq2, lk2, gain, qt, k, vt)


def _conv_kernel(prev_ref, cur_ref, next_ref, w_ref, b_ref, lg_ref, lb_ref, o_ref, ext_ref,
                 *, row_chunk, norm_chunk):
    i = pl.program_id(1)
    tm = cur_ref.shape[0]
    pad = (CONV_WIDTH - 1) // 2
    ext_ref[0:HALO_ROWS] = jnp.where(i > 0, prev_ref[...], 0.0)
    ext_ref[HALO_ROWS:HALO_ROWS + tm] = cur_ref[...]
    ext_ref[HALO_ROWS + tm:] = jnp.where(i < pl.num_programs(1) - 1, next_ref[...], 0.0)
    bias, ln_g, ln_b = b_ref[...], lg_ref[...], lb_ref[...]

    def channel_sum(x):
        return jnp.sum(jnp.sum(x, axis=2, keepdims=True), axis=1, keepdims=True)

    def conv_body(r, carry):
        r0 = r * row_chunk
        acc = jnp.broadcast_to(bias, (row_chunk,) + bias.shape)
        for t in range(CONV_WIDTH):
            off = HALO_ROWS - pad + t
            acc = acc + w_ref[t] * ext_ref[pl.ds(r0 + off, row_chunk)]
        o_ref[pl.ds(r0, row_chunk)] = acc
        return carry

    lax.fori_loop(0, tm // row_chunk, conv_body, 0)

    def norm_body(r, carry):
        rows = pl.ds(r * norm_chunk, norm_chunk)
        y = o_ref[rows]
        xc = y - channel_sum(y) * (1.0 / CONV_CH)
        var = channel_sum(xc * xc) * (1.0 / CONV_CH)
        y = xc * lax.rsqrt(var + EPS) * ln_g + ln_b
        o_ref[rows] = y * jax.nn.sigmoid(y)
        return carry

    lax.fori_loop(0, tm // norm_chunk, norm_body, 0)


def _halo_specs(tm, seq, width, col_map):
    per_tile = tm // HALO_ROWS
    last = seq // HALO_ROWS - 1
    prev = pl.BlockSpec((None, HALO_ROWS, width),
                        lambda b, i, *f: (b, jnp.maximum(i * per_tile - 1, 0), col_map(*f)))
    nxt = pl.BlockSpec((None, HALO_ROWS, width),
                       lambda b, i, *f: (b, jnp.minimum((i + 1) * per_tile, last), col_map(*f)))
    return prev, nxt


def _conv_module(glu, conv_dw_w, conv_dw_b, conv_ln_g, conv_ln_b):
    batch, seq, _ = glu.shape
    tm = _row_tile(seq)
    tile = (CONV_CH // 128, 128)
    per_tile = tm // HALO_ROWS
    last = seq // HALO_ROWS - 1
    vec = pl.BlockSpec(tile, lambda b, i: (0, 0))
    glu4 = glu.reshape((batch, seq) + tile)
    out = pl.pallas_call(
        functools.partial(_conv_kernel, row_chunk=CONV_ROW_CHUNK, norm_chunk=NORM_ROW_CHUNK),
        grid=(batch, seq // tm),
        in_specs=[
            pl.BlockSpec((None, HALO_ROWS) + tile,
                         lambda b, i: (b, jnp.maximum(i * per_tile - 1, 0), 0, 0)),
            pl.BlockSpec((None, tm) + tile, lambda b, i: (b, i, 0, 0)),
            pl.BlockSpec((None, HALO_ROWS) + tile,
                         lambda b, i: (b, jnp.minimum((i + 1) * per_tile, last), 0, 0)),
            pl.BlockSpec((CONV_WIDTH,) + tile, lambda b, i: (0, 0, 0)),
            vec, vec, vec,
        ],
        out_specs=pl.BlockSpec((None, tm) + tile, lambda b, i: (b, i, 0, 0)),
        out_shape=jax.ShapeDtypeStruct((batch, seq) + tile, F32),
        scratch_shapes=[pltpu.VMEM((tm + 2 * HALO_ROWS,) + tile, F32)],
        compiler_params=_params("parallel", "parallel"),
        name="conv_module",
    )(glu4, glu4, glu4, conv_dw_w.reshape((CONV_WIDTH,) + tile), conv_dw_b.reshape(tile),
      conv_ln_g.reshape(tile), conv_ln_b.reshape(tile))
    return out.reshape(batch, seq, CONV_CH)


def _merge_kernel(o_ref, c_ref, g_ref, wa_ref, wc_ref, m_ref):
    a = jnp.dot(o_ref[...], wa_ref[...], preferred_element_type=F32)
    cb = jnp.dot(c_ref[...].astype(BF16), wc_ref[...], preferred_element_type=F32)
    m = g_ref[:, :D_MODEL].astype(F32) * a + g_ref[:, D_MODEL:].astype(F32) * cb
    m_ref[...] = m.astype(BF16)


def _merge(o2, c2, gates, w_attn_proj, w_conv_proj, tm):
    rows = o2.shape[0]
    return pl.pallas_call(
        _merge_kernel,
        grid=(rows // tm,),
        in_specs=[
            pl.BlockSpec((tm, V_WIDTH), lambda i: (i, 0)),
            pl.BlockSpec((tm, CONV_CH), lambda i: (i, 0)),
            pl.BlockSpec((tm, N_BRANCH * D_MODEL), lambda i: (i, 0)),
            _resident((V_WIDTH, D_MODEL)),
            _resident((CONV_CH, D_MODEL)),
        ],
        out_specs=pl.BlockSpec((tm, D_MODEL), lambda i: (i, 0)),
        out_shape=jax.ShapeDtypeStruct((rows, D_MODEL), BF16),
        compiler_params=_params("parallel"),
        name="merge",
    )(o2, c2, gates, w_attn_proj, w_conv_proj)


def _out_proj_kernel(x_ref, m_ref, w_ref, y_ref):
    y_ref[...] = x_ref[...] + jnp.dot(m_ref[...], w_ref[...], preferred_element_type=F32)


def _out_proj(x2, m2, w_out, tm):
    rows = x2.shape[0]
    return pl.pallas_call(
        _out_proj_kernel,
        grid=(rows // tm,),
        in_specs=[
            pl.BlockSpec((tm, D_MODEL), lambda i: (i, 0)),
            pl.BlockSpec((tm, D_MODEL), lambda i: (i, 0)),
            _resident((D_MODEL, D_MODEL)),
        ],
        out_specs=pl.BlockSpec((tm, D_MODEL), lambda i: (i, 0)),
        out_shape=jax.ShapeDtypeStruct((rows, D_MODEL), F32),
        compiler_params=_params("parallel"),
        name="out_proj",
    )(x2, m2, w_out)


def _ffn_up_kernel(x_ref, g_ref, w_ref, up_ref, h_ref):
    @pl.when(pl.program_id(1) == 0)
    def _():
        x = x_ref[...]
        ms = jnp.mean(x * x, axis=-1, keepdims=True)
        h_ref[...] = (x * lax.rsqrt(ms + EPS) * g_ref[...]).astype(BF16)

    up_ref[...] = jnp.dot(h_ref[...], w_ref[...], preferred_element_type=F32).astype(BF16)


def _ffn_up(x2, norm2_g, w_up, tm):
    rows = x2.shape[0]
    tn = FFN_UP_COL_TILE
    return pl.pallas_call(
        _ffn_up_kernel,
        grid=(rows // tm, 2 * FFN_DIM // tn),
        in_specs=[
            pl.BlockSpec((tm, D_MODEL), lambda i, j: (i, 0)),
            pl.BlockSpec((1, D_MODEL), lambda i, j: (0, 0)),
            pl.BlockSpec((D_MODEL, tn), lambda i, j: (0, j)),
        ],
        out_specs=pl.BlockSpec((tm, tn), lambda i, j: (i, j)),
        out_shape=jax.ShapeDtypeStruct((rows, 2 * FFN_DIM), BF16),
        scratch_shapes=[pltpu.VMEM((tm, D_MODEL), BF16)],
        compiler_params=_params("parallel", "arbitrary"),
        name="ffn_up",
    )(x2, norm2_g, w_up)


def _conv3(prev_ref, cur_ref, next_ref, w_ref, b_ref, first, last):
    cur = cur_ref[...].astype(F32)
    tm = cur.shape[0]
    row = lax.broadcasted_iota(jnp.int32, cur.shape, 0)
    before = jnp.where(first, 0.0, prev_ref[HALO_ROWS - 1:HALO_ROWS, :].astype(F32))
    after = jnp.where(last, 0.0, next_ref[0:1, :].astype(F32))
    up_m1 = jnp.where(row == 0, before, pltpu.roll(cur, 1, 0))
    up_p1 = jnp.where(row == tm - 1, after, pltpu.roll(cur, tm - 1, 0))
    return w_ref[0:1, :] * up_m1 + w_ref[1:2, :] * cur + w_ref[2:3, :] * up_p1 + b_ref[...]


def _ffn_down_kernel(pa_ref, a_ref, na_ref, pb_ref, b_ref, nb_ref, wa_ref, ba_ref, wb_ref, bb_ref,
                     wd_ref, x_ref, y_ref):
    i = pl.program_id(1)
    first = i == 0
    last = i == pl.num_programs(1) - 1
    fa = _conv3(pa_ref, a_ref, na_ref, wa_ref, ba_ref, first, last)
    fb = _conv3(pb_ref, b_ref, nb_ref, wb_ref, bb_ref, first, last)
    act = (fa * jax.nn.sigmoid(fa) * fb).astype(BF16)
    y_ref[...] = x_ref[...] + jnp.dot(act, wd_ref[...], preferred_element_type=F32)


def _ffn_down(up, x1, ffn_dw_w, ffn_dw_b, w_down):
    batch, seq, _ = up.shape
    tm = min(FFN_DOWN_ROW_TILE, seq)
    prev_a, next_a = _halo_specs(tm, seq, FFN_DIM, lambda: 0)
    prev_b, next_b = _halo_specs(tm, seq, FFN_DIM, lambda: 1)
    return pl.pallas_call(
        _ffn_down_kernel,
        grid=(batch, seq // tm),
        in_specs=[
            prev_a, pl.BlockSpec((None, tm, FFN_DIM), lambda b, i: (b, i, 0)), next_a,
            prev_b, pl.BlockSpec((None, tm, FFN_DIM), lambda b, i: (b, i, 1)), next_b,
            pl.BlockSpec((FFN_CONV_WIDTH, FFN_DIM), lambda b, i: (0, 0)),
            pl.BlockSpec((1, FFN_DIM), lambda b, i: (0, 0)),
            pl.BlockSpec((FFN_CONV_WIDTH, FFN_DIM), lambda b, i: (0, 1)),
            pl.BlockSpec((1, FFN_DIM), lambda b, i: (0, 1)),
            _resident((FFN_DIM, D_MODEL)),
            pl.BlockSpec((None, tm, D_MODEL), lambda b, i: (b, i, 0)),
        ],
        out_specs=pl.BlockSpec((None, tm, D_MODEL), lambda b, i: (b, i, 0)),
        out_shape=jax.ShapeDtypeStruct((batch, seq, D_MODEL), F32),
        compiler_params=_params("parallel", "parallel"),
        name="ffn_down",
    )(up, up, up, up, up, up, ffn_dw_w, ffn_dw_b, ffn_dw_w, ffn_dw_b, w_down, x1)


def _rope_tables(seq):
    inv_freq = 1.0 / (ROPE_THETA ** (jnp.arange(0, HEAD_DIM, 2, dtype=F32) / HEAD_DIM))
    ang = jnp.arange(seq, dtype=F32)[:, None] * inv_freq[None, :]
    ang = jnp.concatenate([ang, ang], axis=-1)
    sign = jnp.where(jnp.arange(HEAD_DIM) < HEAD_DIM // 2, -1.0, 1.0).astype(F32)
    return jnp.cos(ang), jnp.sin(ang) * sign


def _encoder_layer(x, layer_idx, p):
    batch, seq, _ = x.shape
    rows = batch * seq
    tm = _row_tile(seq)
    lam_init = 0.8 - 0.6 * math.exp(-0.3 * layer_idx)
    cos, sin_signed = _rope_tables(seq)
    x2 = x.reshape(rows, D_MODEL)

    qt, k, vt, glu, gates = _in_proj(x2, seq, p["norm1_g"], p["w_in"], p["b_gate"], p["q_norm_g"],
                                     p["k_norm_g"], cos, sin_signed)
    o = _attention(qt, k.reshape(batch, seq, QK_WIDTH), vt, batch, seq, p["lambda_q1"],
                   p["lambda_k1"], p["lambda_q2"], p["lambda_k2"], p["subln_g"], lam_init)
    c = _conv_module(glu.reshape(batch, seq, CONV_CH), p["conv_dw_w"], p["conv_dw_b"],
                     p["conv_ln_g"], p["conv_ln_b"])
    m = _merge(o.reshape(rows, V_WIDTH), c.reshape(rows, CONV_CH), gates, p["w_attn_proj"],
               p["w_conv_proj"], tm)
    x1 = _out_proj(x2, m, p["w_out"], tm)
    up = _ffn_up(x1, p["norm2_g"], p["w_up"], tm)
    y = _ffn_down(up.reshape(batch, seq, 2 * FFN_DIM), x1.reshape(batch, seq, D_MODEL),
                  p["ffn_dw_w"], p["ffn_dw_b"], p["w_down"])
    return y


_MATRICES = ("w_in", "w_attn_proj", "w_conv_proj", "w_out", "w_up", "w_down")
_ROW_VECTORS = ("norm1_g", "b_gate", "q_norm_g", "k_norm_g", "lambda_q1", "lambda_k1", "lambda_q2",
                "lambda_k2", "subln_g", "conv_dw_b", "conv_ln_g", "conv_ln_b", "norm2_g", "ffn_dw_b")


def kernel(x_prompt, x_sample, norm1_g, w_in, b_gate, q_norm_g, k_norm_g, lambda_q1, lambda_k1, lambda_q2, lambda_k2, subln_g, w_attn_proj, conv_dw_w, conv_dw_b, conv_ln_g, conv_ln_b, w_conv_proj, w_out, norm2_g, w_up, ffn_dw_w, ffn_dw_b, w_down):
    stacked = dict(norm1_g=norm1_g, w_in=w_in, b_gate=b_gate, q_norm_g=q_norm_g, k_norm_g=k_norm_g,
                   lambda_q1=lambda_q1, lambda_k1=lambda_k1, lambda_q2=lambda_q2,
                   lambda_k2=lambda_k2, subln_g=subln_g, w_attn_proj=w_attn_proj,
                   conv_dw_w=conv_dw_w, conv_dw_b=conv_dw_b, conv_ln_g=conv_ln_g,
                   conv_ln_b=conv_ln_b, w_conv_proj=w_conv_proj, w_out=w_out, norm2_g=norm2_g,
                   w_up=w_up, ffn_dw_w=ffn_dw_w, ffn_dw_b=ffn_dw_b, w_down=w_down)
    depth = w_in.shape[0]
    layers = []
    for l in range(depth):
        p = {}
        for name, value in stacked.items():
            value = value[l]
            if name in _MATRICES:
                value = value.astype(BF16)
            elif name in _ROW_VECTORS:
                value = value.astype(F32)[None, :]
            else:
                value = value.astype(F32)
            p[name] = value
        layers.append(p)

    def run(x):
        for l, p in enumerate(layers):
            x = _encoder_layer(x, l, p)
        return x

    return run(x_prompt), run(x_sample)
```
